```python
import math, functools
import jax, jax.numpy as jnp
from jax import lax
import numpy as np

D_MODEL = 1024
BATCH = 2
SEQ = 8192
DEPTH = 2
DEC_BATCH = 32
DEC_SEQ = 4
PAST_LEN = 8192
PAGE_SIZE = 128

N_BRANCH = 4
BR_WIDTH = D_MODEL // 4
CHUNK = 128
A_GROUPS = 4
A_GDIM = BR_WIDTH // A_GROUPS
SC_WIDTH = 3
CF_WIDTH = 31
SB_HEADS = 4
SB_HDIM = BR_WIDTH // SB_HEADS
SB_BIAS_INIT = -6.0
Q_BLOCK = 128
D_FF = 2816
FFN_CONV = 3
PLE_DIM = 256
EPS = 1e-6

A_COLS = 2 * BR_WIDTH
B_COLS = 3 * BR_WIDTH
C_COLS = 2 * BR_WIDTH
D_COLS = 3 * BR_WIDTH
G_COLS = N_BRANCH * D_MODEL
IN_COLS = A_COLS + B_COLS + C_COLS + D_COLS + G_COLS
SPLITS = (A_COLS, A_COLS + B_COLS, A_COLS + B_COLS + C_COLS, A_COLS + B_COLS + C_COLS + D_COLS)

kernel_name = "hybrid_gated_branch_decoder_step"


def rmsnorm(x, g):
    xf = x.astype(jnp.float32)
    y = xf * lax.rsqrt(jnp.mean(xf * xf, axis=-1, keepdims=True) + EPS)
    return (y * g.astype(jnp.float32)).astype(x.dtype)


def layernorm(x, g, b):
    xf = x.astype(jnp.float32)
    mu = jnp.mean(xf, axis=-1, keepdims=True)
    var = jnp.mean(jnp.square(xf - mu), axis=-1, keepdims=True)
    y = (xf - mu) * lax.rsqrt(var + EPS)
    return (y * g.astype(jnp.float32) + b.astype(jnp.float32)).astype(x.dtype)


def causal_dwconv(x, w, prev):
    K = w.shape[0]
    xp = jnp.concatenate([prev.astype(x.dtype), x], axis=1)
    y = lax.conv_general_dilated(xp, w[:, None, :].astype(x.dtype), window_strides=(1,), padding='VALID',
                                 dimension_numbers=('NWC', 'WIO', 'NWC'), feature_group_count=x.shape[-1])
    return y, xp[:, xp.shape[1] - (K - 1):]


def chunk_mix(v, w_sp, b_sp, T):
    Bn, L, _ = v.shape
    vc = v.reshape(Bn, L // T, T, A_GROUPS, A_GDIM)
    w = jnp.where(jnp.tril(jnp.ones((T, T), bool)), w_sp[:, :T, :T], 0.0).astype(v.dtype)
    s = jnp.einsum('gts,bcsgd->bctgd', w, vc) + b_sp[:, :T].T[:, :, None].astype(v.dtype)
    return s.reshape(Bn, L, BR_WIDTH)


def stick_breaking(q, k, v, q_pos, bias):
    z = jnp.einsum('bqhd,bkhd->bhqk', q, k).astype(jnp.float32) * (SB_HDIM ** -0.5) \
        + bias.astype(jnp.float32)[None, :, None, None]
    k_pos = jnp.arange(k.shape[1])
    mask = k_pos[None, :] < q_pos[:, None]
    log_fail = jnp.where(mask, jax.nn.log_sigmoid(-z), 0.0)
    later = lax.cumsum(log_fail, axis=3, reverse=True) - log_fail
    a = jnp.where(mask, jnp.exp(jax.nn.log_sigmoid(z) + later), 0.0)
    return jnp.einsum('bhqk,bkhd->bqhd', a.astype(v.dtype), v)


def sb_prompt(q, k, v, bias):
    Bn, S = q.shape[:2]
    def blk(i):
        qb = lax.dynamic_slice_in_dim(q, i * Q_BLOCK, Q_BLOCK, axis=1)
        return stick_breaking(qb, k, v, i * Q_BLOCK + jnp.arange(Q_BLOCK), bias)
    o = lax.map(blk, jnp.arange(S // Q_BLOCK))
    return jnp.moveaxis(o, 0, 1).reshape(Bn, S, SB_HEADS, SB_HDIM)


def sb_sample(q, k, v, bias, cache_k, cache_v, page_table):
    Bn, T = q.shape[:2]
    past_k = cache_k[page_table].reshape(Bn, -1, SB_HEADS, SB_HDIM)
    past_v = cache_v[page_table].reshape(Bn, -1, SB_HEADS, SB_HDIM)
    past_len = past_k.shape[1]
    k_all = jnp.concatenate([past_k.astype(k.dtype), k], axis=1)
    v_all = jnp.concatenate([past_v.astype(v.dtype), v], axis=1)
    return stick_breaking(q, k_all, v_all, past_len + jnp.arange(T), bias)


def _layer(x, p, prev_b, prev_c, prev_f, attend, chunk_len,
           g_mix, w_in, ln_v_g, ln_v_b, w_sp, b_sp, conv_b_w, conv_c_w, ln_c_g, ln_c_b, sb_bias,
           w_br, w_o, g_ffn, w_up, ffn_conv_w, w_down, g_ple, w_ple_gate, w_ple):
    Bn, T, _ = x.shape
    h = rmsnorm(x, g_mix)
    a_in, b_in, c_in, d_in, gate_in = jnp.split(h @ w_in, SPLITS, axis=-1)
    u, va = jnp.split(jax.nn.gelu(a_in), 2, axis=-1)
    va = layernorm(va, ln_v_g, ln_v_b)
    out_a = u * chunk_mix(va, w_sp, b_sp, chunk_len)
    bg, cg, xb = jnp.split(b_in, 3, axis=-1)
    conv_b, new_b = causal_dwconv(cg * xb, conv_b_w, prev_b)
    out_b = bg * conv_b
    ca, cgate = jnp.split(c_in, 2, axis=-1)
    conv_c, new_c = causal_dwconv(ca * jax.nn.sigmoid(cgate), conv_c_w, prev_c)
    out_c = jax.nn.silu(layernorm(conv_c, ln_c_g, ln_c_b))
    q, k, v = [t.reshape(Bn, T, SB_HEADS, SB_HDIM) for t in jnp.split(d_in, 3, axis=-1)]
    out_d = attend(q, k, v, sb_bias).reshape(Bn, T, BR_WIDTH)
    branches = jnp.stack([out_a, out_b, out_c, out_d], axis=2)
    br = jnp.einsum('btkc,kcd->btkd', branches, w_br)
    gates = jax.nn.sigmoid(gate_in).reshape(Bn, T, N_BRANCH, D_MODEL)
    x = x + jnp.sum(gates * br, axis=2) @ w_o
    up, new_f = causal_dwconv(rmsnorm(x, g_ffn) @ w_up, ffn_conv_w, prev_f)
    ga, gb = jnp.split(up, 2, axis=-1)
    x = x + (jax.nn.silu(ga) * gb) @ w_down
    x = x + jax.nn.sigmoid(rmsnorm(x, g_ple) @ w_ple_gate) * (p @ w_ple)
    return x, k, v, va, new_b, new_c, new_f


def setup_inputs(seed: int = 0) -> dict:
    key = jax.random.key(seed)
    ks = jax.random.split(key, 40)
    n_pages = PAST_LEN // PAGE_SIZE
    n_used = DEC_BATCH * n_pages
    n_phys = n_used + n_used // 4

    def nrm(k, shape, scale=1.0):
        return scale * jax.random.normal(k, shape, jnp.float32)

    def gain(k, shape):
        return 1.0 + 0.1 * jax.random.normal(k, shape, jnp.float32)

    page_table = jax.random.permutation(ks[7], n_phys)[:n_used].reshape(DEC_BATCH, n_pages).astype(jnp.int32)
    return {
        'x_prompt': nrm(ks[0], (BATCH, SEQ, D_MODEL)),
        'x_sample': nrm(ks[1], (DEC_BATCH, DEC_SEQ, D_MODEL)),
        'p_prompt': nrm(ks[2], (DEPTH, BATCH, SEQ, PLE_DIM)),
        'p_sample': nrm(ks[3], (DEPTH, DEC_BATCH, DEC_SEQ, PLE_DIM)),
        'cache_k': nrm(ks[4], (DEPTH, n_phys, PAGE_SIZE, SB_HEADS, SB_HDIM)),
        'cache_v': nrm(ks[5], (DEPTH, n_phys, PAGE_SIZE, SB_HEADS, SB_HDIM)),
        'page_table': page_table,
        'state_conv_b': nrm(ks[8], (DEPTH, DEC_BATCH, SC_WIDTH - 1, BR_WIDTH), 0.5),
        'state_conv_c': nrm(ks[9], (DEPTH, DEC_BATCH, CF_WIDTH - 1, BR_WIDTH), 0.5),
        'state_ffn_conv': nrm(ks[10], (DEPTH, DEC_BATCH, FFN_CONV - 1, 2 * D_FF)),
        'g_mix': gain(ks[11], (DEPTH, D_MODEL)),
        'w_in': nrm(ks[12], (DEPTH, D_MODEL, IN_COLS), D_MODEL ** -0.5),
        'ln_v_g': gain(ks[13], (DEPTH, BR_WIDTH)),
        'ln_v_b': nrm(ks[14], (DEPTH, BR_WIDTH), 0.02),
        'w_sp': nrm(ks[15], (DEPTH, A_GROUPS, CHUNK, CHUNK), CHUNK ** -0.5),
        'b_sp': gain(ks[16], (DEPTH, A_GROUPS, CHUNK)),
        'conv_b_w': nrm(ks[17], (DEPTH, SC_WIDTH, BR_WIDTH), SC_WIDTH ** -0.5),
        'conv_c_w': nrm(ks[18], (DEPTH, CF_WIDTH, BR_WIDTH), CF_WIDTH ** -0.5),
        'ln_c_g': gain(ks[19], (DEPTH, BR_WIDTH)),
        'ln_c_b': nrm(ks[20], (DEPTH, BR_WIDTH), 0.02),
        'sb_bias': SB_BIAS_INIT + 0.1 * jax.random.normal(ks[31], (DEPTH, SB_HEADS), jnp.float32),
        'w_br': nrm(ks[21], (DEPTH, N_BRANCH, BR_WIDTH, D_MODEL), BR_WIDTH ** -0.5),
        'w_o': nrm(ks[22], (DEPTH, D_MODEL, D_MODEL), D_MODEL ** -0.5),
        'g_ffn': gain(ks[23], (DEPTH, D_MODEL)),
        'w_up': nrm(ks[24], (DEPTH, D_MODEL, 2 * D_FF), D_MODEL ** -0.5),
        'ffn_conv_w': nrm(ks[25], (DEPTH, FFN_CONV, 2 * D_FF), FFN_CONV ** -0.5),
        'w_down': nrm(ks[26], (DEPTH, D_FF, D_MODEL), D_FF ** -0.5),
        'g_ple': gain(ks[27], (DEPTH, D_MODEL)),
        'w_ple_gate': nrm(ks[28], (DEPTH, D_MODEL, D_MODEL), D_MODEL ** -0.5),
        'w_ple': nrm(ks[29], (DEPTH, PLE_DIM, D_MODEL), PLE_DIM ** -0.5),
        'g_final': gain(ks[30], (D_MODEL,)),
    }


def reference(x_prompt, x_sample, p_prompt, p_sample, cache_k, cache_v, page_table,
              state_conv_b, state_conv_c, state_ffn_conv,
              g_mix, w_in, ln_v_g, ln_v_b, w_sp, b_sp, conv_b_w, conv_c_w, ln_c_g, ln_c_b, sb_bias,
              w_br, w_o, g_ffn, w_up, ffn_conv_w, w_down, g_ple, w_ple_gate, w_ple, g_final):
    xp, xs = x_prompt, x_sample
    Bp, Bs, Ts = x_prompt.shape[0], x_sample.shape[0], x_sample.shape[1]
    kp, vp, cbp, ccp, cfp = [], [], [], [], []
    ks_, vs_, cvs, cbs, ccs, cfs = [], [], [], [], [], []
    for i in range(DEPTH):
        prm = (g_mix[i], w_in[i], ln_v_g[i], ln_v_b[i], w_sp[i], b_sp[i], conv_b_w[i], conv_c_w[i],
               ln_c_g[i], ln_c_b[i], sb_bias[i], w_br[i], w_o[i], g_ffn[i], w_up[i], ffn_conv_w[i], w_down[i],
               g_ple[i], w_ple_gate[i], w_ple[i])
        zb = jnp.zeros((Bp, SC_WIDTH - 1, BR_WIDTH), xp.dtype)
        zc = jnp.zeros((Bp, CF_WIDTH - 1, BR_WIDTH), xp.dtype)
        zf = jnp.zeros((Bp, FFN_CONV - 1, 2 * D_FF), xp.dtype)
        xp, k_, v_, _, nb, nc, nf = _layer(xp, p_prompt[i], zb, zc, zf, sb_prompt, CHUNK, *prm)
        kp.append(k_); vp.append(v_); cbp.append(nb); ccp.append(nc); cfp.append(nf)
        attend = functools.partial(sb_sample, cache_k=cache_k[i], cache_v=cache_v[i], page_table=page_table)
        xs, k_, v_, va, nb, nc, nf = _layer(xs, p_sample[i], state_conv_b[i], state_conv_c[i],
                                            state_ffn_conv[i], attend, Ts, *prm)
        ks_.append(k_); vs_.append(v_); cvs.append(va); cbs.append(nb); ccs.append(nc); cfs.append(nf)
    y_prompt = rmsnorm(xp, g_final)
    y_sample = rmsnorm(xs, g_final)
    return (y_prompt, y_sample,
            jnp.stack(kp), jnp.stack(vp), jnp.stack(cbp), jnp.stack(ccp), jnp.stack(cfp),
            jnp.stack(ks_), jnp.stack(vs_), jnp.stack(cvs), jnp.stack(cbs), jnp.stack(ccs), jnp.stack(cfs))
```

```python
import functools

import jax
import jax.numpy as jnp
from jax import lax
from jax.experimental import pallas as pl
from jax.experimental.pallas import tpu as pltpu

F32 = jnp.float32
BF16 = jnp.bfloat16

EPS = 1e-6
BR_WIDTH = 256
N_HEADS = 4
HEAD_DIM = 64
MIX_COLS = 10 * BR_WIDTH
CONV_B_TAPS = 3
CONV_C_TAPS = 31
FFN_TAPS = 3
FFN_CHUNK = 256
SB_KEY_BLOCK = 256
PAGES_PER_STEP = 8
VMEM_LIMIT = 56 * 1024 * 1024


def _rms(x, g):
    return x * lax.rsqrt(jnp.mean(x * x, axis=-1, keepdims=True) + EPS) * g


def _layernorm(x, g, b):
    mu = jnp.mean(x, axis=-1, keepdims=True)
    xc = x - mu
    var = jnp.mean(xc * xc, axis=-1, keepdims=True)
    return xc * lax.rsqrt(var + EPS) * g + b


def _bdot(a, b):
    return jnp.dot(a, b, preferred_element_type=F32)


def _const_spec(shape):
    nd = len(shape)
    return pl.BlockSpec(shape, lambda *_: (0,) * nd, pipeline_mode=pl.Buffered(1))


def _mixer_body(x_ref, gmix_ref, wmix_ref, lnvg_ref, lnvb_ref, wsp_ref, bsp_ref,
                cbw_ref, ccw_ref, lncg_ref, lncb_ref, stb_ref, stc_ref,
                oa_ref, ob_ref, oc_ref, q_ref, k_ref, v_ref, kb_ref, vb_ref,
                va_ref, nb_ref, nc_ref, bbuf, cbuf, *, sample, tm, sh, hb, hc):
    i = pl.program_id(1)

    @pl.when(i == 0)
    def _():
        bbuf[0:hb, :] = stb_ref[0]
        cbuf[0:hc, :] = stc_ref[0]

    w = BR_WIDTH
    h = _rms(x_ref[0], gmix_ref[...]).astype(BF16)
    y = _bdot(h, wmix_ref[...])

    ga = jax.nn.gelu(y[:, 0:2 * w])
    u = ga[:, 0:w]
    va = _layernorm(ga[:, w:2 * w], lnvg_ref[...], lnvb_ref[...])
    va_ref[0] = va
    parts = []
    if sample:
        nt = tm // sh
        for t in range(nt):
            s = jnp.broadcast_to(bsp_ref[t:t + 1, :], (sh, w))
            for s2 in range(t + 1):
                s = s + wsp_ref[t * nt + s2:t * nt + s2 + 1, :] * va[s2 * sh:(s2 + 1) * sh, :]
            parts.append(s)
    else:
        ch = wsp_ref.shape[1]
        rr = lax.broadcasted_iota(jnp.int32, (ch, ch), 0)
        cc = lax.broadcasted_iota(jnp.int32, (ch, ch), 1)
        grp = lax.broadcasted_iota(jnp.int32, (ch, w), 1) // (w // wsp_ref.shape[0])
        wg = [jnp.where(cc <= rr, wsp_ref[g], 0.0).astype(BF16) for g in range(wsp_ref.shape[0])]
        for c in range(tm // ch):
            vc = va[c * ch:(c + 1) * ch, :]
            s = bsp_ref[...]
            for g in range(wsp_ref.shape[0]):
                s = s + _bdot(wg[g], jnp.where(grp == g, vc, 0.0).astype(BF16))
            parts.append(s)
    oa_ref[0] = u * jnp.concatenate(parts, axis=0)

    bbuf[hb:hb + tm, :] = y[:, 3 * w:4 * w] * y[:, 4 * w:5 * w]
    base = hb - (CONV_B_TAPS - 1) * sh
    cb = cbw_ref[0:1, :] * bbuf[base:base + tm, :]
    for j in range(1, CONV_B_TAPS):
        cb = cb + cbw_ref[j:j + 1, :] * bbuf[base + j * sh:base + j * sh + tm, :]
    ob_ref[0] = y[:, 2 * w:3 * w] * cb
    nb_ref[0] = bbuf[hb + tm - (CONV_B_TAPS - 1) * sh:hb + tm, :]
    bbuf[0:hb, :] = bbuf[tm:tm + hb, :]

    cbuf[hc:hc + tm, :] = y[:, 5 * w:6 * w] * jax.nn.sigmoid(y[:, 6 * w:7 * w])
    base = hc - (CONV_C_TAPS - 1) * sh
    cv = ccw_ref[0:1, :] * cbuf[base:base + tm, :]
    for j in range(1, CONV_C_TAPS):
        cv = cv + ccw_ref[j:j + 1, :] * cbuf[base + j * sh:base + j * sh + tm, :]
    oc_ref[0] = jax.nn.silu(_layernorm(cv, lncg_ref[...], lncb_ref[...]))
    nc_ref[0] = cbuf[hc + tm - (CONV_C_TAPS - 1) * sh:hc + tm, :]
    cbuf[0:hc, :] = cbuf[tm:tm + hc, :]

    q = y[:, 7 * w:8 * w]
    k = y[:, 8 * w:9 * w]
    v = y[:, 9 * w:10 * w]
    q_ref[0] = q
    k_ref[0] = k
    v_ref[0] = v
    kb_ref[0] = k.astype(BF16)
    vb_ref[0] = v.astype(BF16)


def _mixer(x, g_mix, w_mix, ln_v_g, ln_v_b, wsp, bsp, conv_b_w, conv_c_w, ln_c_g, ln_c_b,
           st_b, st_c, *, sample, tm, sh):
    B, S, D = x.shape
    w = BR_WIDTH
    hb, hc = st_b.shape[1], st_c.shape[1]
    nt = S // tm
    tok = lambda width: pl.BlockSpec((1, tm, width), lambda b, i: (b, i, 0))
    per_seq = lambda rows, width: pl.BlockSpec((1, rows, width), lambda b, i: (b, 0, 0))
    row = lambda a: a.reshape(1, -1)
    consts = [row(g_mix), w_mix, row(ln_v_g), row(ln_v_b), wsp, bsp, conv_b_w, conv_c_w,
              row(ln_c_g), row(ln_c_b)]
    f32_tok = jax.ShapeDtypeStruct((B, S, w), F32)
    bf_tok = jax.ShapeDtypeStruct((B, S, w), BF16)
    nb_rows, nc_rows = (CONV_B_TAPS - 1) * sh, (CONV_C_TAPS - 1) * sh
    return pl.pallas_call(
        functools.partial(_mixer_body, sample=sample, tm=tm, sh=sh, hb=hb, hc=hc),
        grid=(B, nt),
        in_specs=[tok(D)] + [_const_spec(c.shape) for c in consts]
                 + [per_seq(hb, w), per_seq(hc, w)],
        out_specs=[tok(w)] * 9 + [per_seq(nb_rows, w), per_seq(nc_rows, w)],
        out_shape=[f32_tok] * 6 + [bf_tok] * 2 + [f32_tok]
                  + [jax.ShapeDtypeStruct((B, nb_rows, w), F32),
                     jax.ShapeDtypeStruct((B, nc_rows, w), F32)],
        scratch_shapes=[pltpu.VMEM((hb + tm, w), F32), pltpu.VMEM((hc + tm, w), F32)],
        compiler_params=pltpu.CompilerParams(
            dimension_semantics=("arbitrary", "arbitrary"), vmem_limit_bytes=VMEM_LIMIT),
        name="mixer_sample" if sample else "mixer_prompt",
    )(x, *consts, st_b, st_c)


def _suffix_matrix():
    n = SB_KEY_BLOCK
    r = lax.broadcasted_iota(jnp.int32, (n, n), 0)
    c = lax.broadcasted_iota(jnp.int32, (n, n), 1)
    return (r > c).astype(BF16)


def _sb_tile(qm, kt, vt, bias_col, run, suffix, mask):
    z = lax.dot_general(qm, kt, (((1,), (1,)), ((), ())), preferred_element_type=F32) + bias_col
    sp = jnp.maximum(z, 0.0) + jnp.log(1.0 + jnp.exp(-jnp.abs(z)))
    if mask is not None:
        sp = jnp.where(mask, sp, 0.0)
    nblk = kt.shape[0] // SB_KEY_BLOCK
    later = [None] * nblk
    for b in reversed(range(nblk)):
        spb = sp[:, b * SB_KEY_BLOCK:(b + 1) * SB_KEY_BLOCK]
        later[b] = _bdot(spb.astype(BF16), suffix) + run
        run = run + jnp.sum(spb, axis=-1, keepdims=True)
    later = later[0] if nblk == 1 else jnp.concatenate(later, axis=1)
    a = jnp.exp(z - sp - later)
    if mask is not None:
        a = jnp.where(mask, a, 0.0)
    return _bdot(a.astype(BF16), vt), run


def _head_stack(q, qm_ref, rows):
    lane_head = lax.broadcasted_iota(jnp.int32, q.shape, 1) // HEAD_DIM
    for h in range(N_HEADS):
        qm_ref[h * rows:(h + 1) * rows, :] = jnp.where(lane_head == h, q, 0.0).astype(BF16)


def _head_unstack(acc_ref, rows):
    lane_head = lax.broadcasted_iota(jnp.int32, (rows, N_HEADS * HEAD_DIM), 1) // HEAD_DIM
    out = jnp.where(lane_head == 0, acc_ref[0:rows, :], 0.0)
    for h in range(1, N_HEADS):
        out = out + jnp.where(lane_head == h, acc_ref[h * rows:(h + 1) * rows, :], 0.0)
    return out


def _attn_prompt_body(q_ref, kb_ref, vb_ref, bias_ref, o_ref, qm_ref, acc_ref, run_ref, *, tq, tk):
    i = pl.program_id(1)
    m = N_HEADS * tq
    _head_stack(q_ref[0] * (HEAD_DIM ** -0.5), qm_ref, tq)
    qm = qm_ref[...]
    bias_col = bias_ref[...]
    suffix = _suffix_matrix()

    jd = (i * tq) // tk
    ks = pl.multiple_of(jd * tk, tk)
    qpos = i * tq + (lax.broadcasted_iota(jnp.int32, (m, tk), 0) & (tq - 1))
    kpos = jd * tk + lax.broadcasted_iota(jnp.int32, (m, tk), 1)
    pv, run = _sb_tile(qm, kb_ref[0, pl.ds(ks, tk), :], vb_ref[0, pl.ds(ks, tk), :], bias_col,
                       jnp.zeros((m, 1), F32), suffix, kpos < qpos)
    acc_ref[...] = pv
    run_ref[...] = run

    def body(n, carry):
        ks = pl.multiple_of((jd - 1 - n) * tk, tk)
        pv, run = _sb_tile(qm, kb_ref[0, pl.ds(ks, tk), :], vb_ref[0, pl.ds(ks, tk), :], bias_col,
                           run_ref[...], suffix, None)
        acc_ref[...] += pv
        run_ref[...] = run
        return carry

    lax.fori_loop(0, jd, body, 0)
    o_ref[0] = _head_unstack(acc_ref, tq)


def _attn_prompt(q, kb, vb, sb_bias, *, tq=128, tk=SB_KEY_BLOCK):
    B, S, w = q.shape
    m = N_HEADS * tq
    bias_rows = jnp.repeat(sb_bias.astype(F32), tq).reshape(m, 1)
    return pl.pallas_call(
        functools.partial(_attn_prompt_body, tq=tq, tk=tk),
        grid=(B, S // tq),
        in_specs=[pl.BlockSpec((1, tq, w), lambda b, i: (b, i, 0)),
                  pl.BlockSpec((1, S, w), lambda b, i: (b, 0, 0)),
                  pl.BlockSpec((1, S, w), lambda b, i: (b, 0, 0)),
                  _const_spec((m, 1))],
        out_specs=pl.BlockSpec((1, tq, w), lambda b, i: (b, i, 0)),
        out_shape=jax.ShapeDtypeStruct((B, S, w), F32),
        scratch_shapes=[pltpu.VMEM((m, w), BF16), pltpu.VMEM((m, w), F32), pltpu.VMEM((m, 1), F32)],
        compiler_params=pltpu.CompilerParams(
            dimension_semantics=("arbitrary", "arbitrary"), vmem_limit_bytes=VMEM_LIMIT),
        name="attend_prompt",
    )(q, kb, vb, bias_rows)


def _attn_sample_body(pt_ref, q_ref, kn_ref, vn_ref, bias_ref, *refs, tpad, page, npg):
    k_pages = refs[0:npg]
    v_pages = refs[npg:2 * npg]
    o_ref, qm_ref, kt_ref, vt_ref, acc_ref, run_ref = refs[2 * npg:]
    g = pl.program_id(1)
    m = N_HEADS * tpad
    bias_col = bias_ref[...]
    suffix = _suffix_matrix()

    @pl.when(g == 0)
    def _():
        _head_stack(q_ref[0] * (HEAD_DIM ** -0.5), qm_ref, tpad)
        nk = SB_KEY_BLOCK
        kt_ref[0:nk, :] = jnp.zeros((nk, kt_ref.shape[1]), BF16)
        vt_ref[0:nk, :] = jnp.zeros((nk, vt_ref.shape[1]), BF16)
        kt_ref[0:tpad, :] = kn_ref[0].astype(BF16)
        vt_ref[0:tpad, :] = vn_ref[0].astype(BF16)
        t_q = lax.broadcasted_iota(jnp.int32, (m, nk), 0) & (tpad - 1)
        t_k = lax.broadcasted_iota(jnp.int32, (m, nk), 1)
        pv, run = _sb_tile(qm_ref[...], kt_ref[0:nk, :], vt_ref[0:nk, :], bias_col,
                           jnp.zeros((m, 1), F32), suffix, t_k < t_q)
        acc_ref[...] = pv
        run_ref[...] = run

    @pl.when(g > 0)
    def _():
        for p in range(npg):
            kt_ref[p * page:(p + 1) * page, :] = k_pages[p][0].astype(BF16)
            vt_ref[p * page:(p + 1) * page, :] = v_pages[p][0].astype(BF16)
        pv, run = _sb_tile(qm_ref[...], kt_ref[...], vt_ref[...], bias_col, run_ref[...], suffix, None)
        acc_ref[...] += pv
        run_ref[...] = run

    @pl.when(g == pl.num_programs(1) - 1)
    def _():
        o_ref[0] = _head_unstack(acc_ref, tpad)


def _attn_sample(q, k_new, v_new, sb_bias, cache_k, cache_v, page_table):
    n_seq, tpad, w = q.shape
    n_phys, page = cache_k.shape[0], cache_k.shape[1]
    n_pages = page_table.shape[1]
    npg = PAGES_PER_STEP
    n_groups = n_pages // npg
    m = N_HEADS * tpad
    ck = cache_k.reshape(n_phys, page, w)
    cv = cache_v.reshape(n_phys, page, w)
    bias_rows = jnp.repeat(sb_bias.astype(F32), tpad).reshape(m, 1)

    def page_spec(p):
        def index(s, g, pt):
            grp = n_groups - jnp.maximum(g, 1)
            return (pt[s * n_pages + grp * npg + p], 0, 0)
        return pl.BlockSpec((1, page, w), index)

    per_seq = pl.BlockSpec((1, tpad, w), lambda s, g, pt: (s, 0, 0))
    grid_spec = pltpu.PrefetchScalarGridSpec(
        num_scalar_prefetch=1,
        grid=(n_seq, n_groups + 1),
        in_specs=[per_seq, per_seq, per_seq, pl.BlockSpec((m, 1), lambda s, g, pt: (0, 0))]
                 + [page_spec(p) for p in range(npg)] * 2,
        out_specs=per_seq,
        scratch_shapes=[pltpu.VMEM((m, w), BF16), pltpu.VMEM((npg * page, w), BF16),
                        pltpu.VMEM((npg * page, w), BF16), pltpu.VMEM((m, w), F32),
                        pltpu.VMEM((m, 1), F32)],
    )
    return pl.pallas_call(
        functools.partial(_attn_sample_body, tpad=tpad, page=page, npg=npg),
        grid_spec=grid_spec,
        out_shape=jax.ShapeDtypeStruct((n_seq, tpad, w), F32),
        compiler_params=pltpu.CompilerParams(
            dimension_semantics=("arbitrary", "arbitrary"), vmem_limit_bytes=VMEM_LIMIT),
        name="attend_sample",
    )(page_table.reshape(-1), q, k_new, v_new, bias_rows, *([ck] * npg), *([cv] * npg))


def _merge_body(x_ref, oa_ref, ob_ref, oc_ref, od_ref, gmix_ref, wgate_ref, wbr_ref, wo_ref, xo_ref):
    x = x_ref[...]
    d = x.shape[1]
    h = _rms(x, gmix_ref[...]).astype(BF16)
    merged = None
    for kk, br_ref in enumerate((oa_ref, ob_ref, oc_ref, od_ref)):
        gate = jax.nn.sigmoid(_bdot(h, wgate_ref[:, kk * d:(kk + 1) * d]))
        term = gate * _bdot(br_ref[...].astype(BF16), wbr_ref[kk])
        merged = term if merged is None else merged + term
    xo_ref[...] = x + _bdot(merged.astype(BF16), wo_ref[...])


def _merge(x, oa, ob, oc, od, g_mix, w_gate, w_br, w_o, *, tm):
    n, d = x.shape
    w = BR_WIDTH
    tok = lambda width: pl.BlockSpec((tm, width), lambda i: (i, 0))
    consts = [g_mix.reshape(1, -1), w_gate, w_br, w_o]
    return pl.pallas_call(
        _merge_body,
        grid=(n // tm,),
        in_specs=[tok(d)] + [tok(w)] * 4 + [_const_spec(c.shape) for c in consts],
        out_specs=tok(d),
        out_shape=jax.ShapeDtypeStruct((n, d), F32),
        compiler_params=pltpu.CompilerParams(
            dimension_semantics=("arbitrary",), vmem_limit_bytes=VMEM_LIMIT),
        name="merge",
    )(x, oa, ob, oc, od, *consts)


def _ffn_body(x_ref, p_ref, st_ref, gffn_ref, wup_ref, cw_ref, wdn_ref, gple_ref, wpg_ref,
              wple_ref, gfin_ref, xo_ref, nf_ref, hbuf, ubuf, acc_ref, *, tm, sh, hh, final):
    i = pl.program_id(1)

    @pl.when(i == 0)
    def _():
        hbuf[...] = st_ref[0]

    x = x_ref[0]
    hn = _rms(x, gffn_ref[...]).astype(BF16)
    d_ff = wdn_ref.shape[0]
    cw = FFN_CHUNK
    base = hh - (FFN_TAPS - 1) * sh
    for c in range(d_ff // cw):
        conv = []
        for half in range(2):
            col = half * d_ff + c * cw
            ubuf[0:hh, half * cw:(half + 1) * cw] = hbuf[:, col:col + cw]
            ubuf[hh:hh + tm, half * cw:(half + 1) * cw] = _bdot(hn, wup_ref[:, col:col + cw])
            cv = cw_ref[0:1, col:col + cw] * ubuf[base:base + tm, half * cw:(half + 1) * cw]
            for j in range(1, FFN_TAPS):
                cv = cv + (cw_ref[j:j + 1, col:col + cw]
                           * ubuf[base + j * sh:base + j * sh + tm, half * cw:(half + 1) * cw])
            conv.append(cv)
            hbuf[:, col:col + cw] = ubuf[tm:tm + hh, half * cw:(half + 1) * cw]
        act = (jax.nn.silu(conv[0]) * conv[1]).astype(BF16)
        contrib = _bdot(act, wdn_ref[c * cw:(c + 1) * cw, :])
        if c == 0:
            acc_ref[...] = contrib
        else:
            acc_ref[...] += contrib
    nf_ref[0] = hbuf[hh - (FFN_TAPS - 1) * sh:hh, :]
    x = x + acc_ref[...]
    gate = jax.nn.sigmoid(_bdot(_rms(x, gple_ref[...]).astype(BF16), wpg_ref[...]))
    x = x + gate * _bdot(p_ref[0].astype(BF16), wple_ref[...])
    xo_ref[0] = _rms(x, gfin_ref[...]) if final else x


def _ffn(x, p, st_f, g_ffn, w_up, ffn_conv_w, w_down, g_ple, w_ple_gate, w_ple, g_final,
         *, tm, sh, final):
    B, S, D = x.shape
    hh, up_cols = st_f.shape[1], st_f.shape[2]
    tok = lambda width: pl.BlockSpec((1, tm, width), lambda b, i: (b, i, 0))
    row = lambda a: a.reshape(1, -1)
    consts = [row(g_ffn), w_up, ffn_conv_w, w_down, row(g_ple), w_ple_gate, w_ple, row(g_final)]
    nf_rows = (FFN_TAPS - 1) * sh
    return pl.pallas_call(
        functools.partial(_ffn_body, tm=tm, sh=sh, hh=hh, final=final),
        grid=(B, S // tm),
        in_specs=[tok(D), tok(p.shape[2]), pl.BlockSpec((1, hh, up_cols), lambda b, i: (b, 0, 0))]
                 + [_const_spec(c.shape) for c in consts],
        out_specs=[tok(D), pl.BlockSpec((1, nf_rows, up_cols), lambda b, i: (b, 0, 0))],
        out_shape=[jax.ShapeDtypeStruct((B, S, D), F32),
                   jax.ShapeDtypeStruct((B, nf_rows, up_cols), F32)],
        scratch_shapes=[pltpu.VMEM((hh, up_cols), F32), pltpu.VMEM((hh + tm, 2 * FFN_CHUNK), F32),
                        pltpu.VMEM((tm, D), F32)],
        compiler_params=pltpu.CompilerParams(
            dimension_semantics=("arbitrary", "arbitrary"), vmem_limit_bytes=VMEM_LIMIT),
        name="ffn",
    )(x, p, st_f, *consts)


def _token_major(a):
    n_seq, t, c = a.shape
    return a.transpose(1, 0, 2).reshape(1, t * n_seq, c)


def _seq_major(a, n_seq):
    c = a.shape[-1]
    return a.reshape(-1, n_seq, c).transpose(1, 0, 2)


def kernel(x_prompt, x_sample, p_prompt, p_sample, cache_k, cache_v, page_table, state_conv_b, state_conv_c, state_ffn_conv, g_mix, w_in, ln_v_g, ln_v_b, w_sp, b_sp, conv_b_w, conv_c_w, ln_c_g, ln_c_b, sb_bias, w_br, w_o, g_ffn, w_up, ffn_conv_w, w_down, g_ple, w_ple_gate, w_ple, g_final):
    depth = w_in.shape[0]
    B, S, D = x_prompt.shape
    n_seq, T, _ = x_sample.shape
    w = BR_WIDTH
    n_grp = w_sp.shape[1]
    chunk = w_sp.shape[2]
    tm = min(512, S)
    tpad = 8
    assert S % tm == 0 and tm % chunk == 0 and S % SB_KEY_BLOCK == 0 and T <= tpad

    xp = x_prompt
    xs = _token_major(x_sample)
    zeros_b = jnp.zeros((B, 8, w), F32)
    zeros_c = jnp.zeros((B, 32, w), F32)
    zeros_f = jnp.zeros((B, 8, w_up.shape[2]), F32)

    outs = {name: [] for name in ("kp", "vp", "cbp", "ccp", "cfp", "ks", "vs", "cvs", "cbs", "ccs", "cfs")}
    for l in range(depth):
        final = l == depth - 1
        w_mix = w_in[l, :, :MIX_COLS].astype(BF16)
        w_gate = w_in[l, :, MIX_COLS:].astype(BF16)
        w_br_l = w_br[l].astype(BF16)
        w_o_l = w_o[l].astype(BF16)
        w_up_l = w_up[l].astype(BF16)
        w_down_l = w_down[l].astype(BF16)
        w_pg_l = w_ple_gate[l].astype(BF16)
        w_ple_l = w_ple[l].astype(BF16)
        grp_cols = w // n_grp

        bmat = jnp.repeat(b_sp[l][:, :chunk].T, grp_cols, axis=1)
        (oa, ob, oc, q, k, v, kb, vb, _, nb, nc) = _mixer(
            xp, g_mix[l], w_mix, ln_v_g[l], ln_v_b[l], w_sp[l], bmat, conv_b_w[l], conv_c_w[l],
            ln_c_g[l], ln_c_b[l], zeros_b, zeros_c, sample=False, tm=tm, sh=1)
        od = _attn_prompt(q, kb, vb, sb_bias[l])
        flat = lambda a: a.reshape(B * S, a.shape[-1])
        x1 = _merge(flat(xp), flat(oa), flat(ob), flat(oc), flat(od), g_mix[l], w_gate, w_br_l,
                    w_o_l, tm=tm).reshape(B, S, D)
        xp, nf = _ffn(x1, p_prompt[l], zeros_f, g_ffn[l], w_up_l, ffn_conv_w[l], w_down_l,
                      g_ple[l], w_pg_l, w_ple_l, g_final, tm=tm, sh=1, final=final)
        outs["kp"].append(k.reshape(B, S, N_HEADS, HEAD_DIM))
        outs["vp"].append(v.reshape(B, S, N_HEADS, HEAD_DIM))
        outs["cbp"].append(nb)
        outs["ccp"].append(nc)
        outs["cfp"].append(nf)

        wv = jnp.repeat(w_sp[l][:, :T, :T].transpose(1, 2, 0).reshape(T * T, n_grp), grp_cols, axis=1)
        bv = jnp.repeat(b_sp[l][:, :T].T, grp_cols, axis=1)
        (oa, ob, oc, q, k, v, _, _, va, nb, nc) = _mixer(
            xs, g_mix[l], w_mix, ln_v_g[l], ln_v_b[l], wv, bv, conv_b_w[l], conv_c_w[l],
            ln_c_g[l], ln_c_b[l], _token_major(state_conv_b[l]), _token_major(state_conv_c[l]),
            sample=True, tm=T * n_seq, sh=n_seq)
        pad_t = lambda a: jnp.pad(_seq_major(a, n_seq), ((0, 0), (0, tpad - T), (0, 0)))
        od = _attn_sample(pad_t(q), pad_t(k), pad_t(v), sb_bias[l], cache_k[l], cache_v[l], page_table)
        od = _token_major(od[:, :T, :])
        x1 = _merge(xs[0], oa[0], ob[0], oc[0], od[0], g_mix[l], w_gate, w_br_l, w_o_l,
                    tm=T * n_seq)[None]
        xs, nf = _ffn(x1, _token_major(p_sample[l]), _token_major(state_ffn_conv[l]), g_ffn[l],
                      w_up_l, ffn_conv_w[l], w_down_l, g_ple[l], w_pg_l, w_ple_l, g_final,
                      tm=T * n_seq, sh=n_seq, final=final)
        outs["ks"].append(_seq_major(k, n_seq).reshape(n_seq, T, N_HEADS, HEAD_DIM))
        outs["vs"].append(_seq_major(v, n_seq).reshape(n_seq, T, N_HEADS, HEAD_DIM))
        outs["cvs"].append(_seq_major(va, n_seq))
        outs["cbs"].append(_seq_major(nb, n_seq))
        outs["ccs"].append(_seq_major(nc, n_seq))
        outs["cfs"].append(_seq_major(nf, n_seq))

    st = lambda name: jnp.stack(outs[name])
    return (xp, _seq_major(xs, n_seq), st("kp"), st("vp"), st("cbp"), st("ccp"), st("cfp"),
            st("ks"), st("vs"), st("cvs"), st("cbs"), st("ccs"), st("cfs"))
```

```python
import functools

import jax
import jax.numpy as jnp
from jax import lax
from jax.experimental import pallas as pl
from jax.experimental.pallas import tpu as pltpu

F32 = jnp.float32
BF16 = jnp.bfloat16

EPS = 1e-6
BR_WIDTH = 256
N_HEADS = 4
HEAD_DIM = 64
MIX_COLS = 10 * BR_WIDTH
CONV_B_TAPS = 3
CONV_C_TAPS = 31
FFN_TAPS = 3
FFN_CHUNK = 256
SB_KEY_BLOCK = 256
PAGES_PER_STEP = 16
VMEM_LIMIT = 56 * 1024 * 1024


def _rms(x, g):
    return x * lax.rsqrt(jnp.mean(x * x, axis=-1, keepdims=True) + EPS) * g


def _layernorm(x, g, b):
    mu = jnp.mean(x, axis=-1, keepdims=True)
    xc = x - mu
    var = jnp.mean(xc * xc, axis=-1, keepdims=True)
    return xc * lax.rsqrt(var + EPS) * g + b


def _bdot(a, b):
    return jnp.dot(a, b, preferred_element_type=F32)


def _const_spec(shape):
    nd = len(shape)
    return pl.BlockSpec(shape, lambda *_: (0,) * nd, pipeline_mode=pl.Buffered(1))


def _mixer_body(x_ref, gmix_ref, wmix_ref, lnvg_ref, lnvb_ref, wsp_ref, bsp_ref,
                cbw_ref, ccw_ref, lncg_ref, lncb_ref, stb_ref, stc_ref,
                oa_ref, ob_ref, oc_ref, q_ref, k_ref, v_ref, kb_ref, vb_ref,
                va_ref, nb_ref, nc_ref, bbuf, cbuf, *, sample, tm, sh, hb, hc):
    i = pl.program_id(1)

    @pl.when(i == 0)
    def _():
        bbuf[0:hb, :] = stb_ref[0]
        cbuf[0:hc, :] = stc_ref[0]

    w = BR_WIDTH
    h = _rms(x_ref[0], gmix_ref[...]).astype(BF16)
    y = _bdot(h, wmix_ref[...])

    ga = jax.nn.gelu(y[:, 0:2 * w])
    u = ga[:, 0:w]
    va = _layernorm(ga[:, w:2 * w], lnvg_ref[...], lnvb_ref[...])
    va_ref[0] = va
    parts = []
    if sample:
        nt = tm // sh
        for t in range(nt):
            s = jnp.broadcast_to(bsp_ref[t:t + 1, :], (sh, w))
            for s2 in range(t + 1):
                s = s + wsp_ref[t * nt + s2:t * nt + s2 + 1, :] * va[s2 * sh:(s2 + 1) * sh, :]
            parts.append(s)
    else:
        ch = wsp_ref.shape[1]
        rr = lax.broadcasted_iota(jnp.int32, (ch, ch), 0)
        cc = lax.broadcasted_iota(jnp.int32, (ch, ch), 1)
        grp = lax.broadcasted_iota(jnp.int32, (ch, w), 1) // (w // wsp_ref.shape[0])
        wg = [jnp.where(cc <= rr, wsp_ref[g], 0.0).astype(BF16) for g in range(wsp_ref.shape[0])]
        for c in range(tm // ch):
            vc = va[c * ch:(c + 1) * ch, :]
            s = bsp_ref[...]
            for g in range(wsp_ref.shape[0]):
                s = s + _bdot(wg[g], jnp.where(grp == g, vc, 0.0).astype(BF16))
            parts.append(s)
    oa_ref[0] = u * jnp.concatenate(parts, axis=0)

    bbuf[hb:hb + tm, :] = y[:, 3 * w:4 * w] * y[:, 4 * w:5 * w]
    base = hb - (CONV_B_TAPS - 1) * sh
    cb = cbw_ref[0:1, :] * bbuf[base:base + tm, :]
    for j in range(1, CONV_B_TAPS):
        cb = cb + cbw_ref[j:j + 1, :] * bbuf[base + j * sh:base + j * sh + tm, :]
    ob_ref[0] = y[:, 2 * w:3 * w] * cb
    nb_ref[0] = bbuf[hb + tm - (CONV_B_TAPS - 1) * sh:hb + tm, :]
    bbuf[0:hb, :] = bbuf[tm:tm + hb, :]

    cbuf[hc:hc + tm, :] = y[:, 5 * w:6 * w] * jax.nn.sigmoid(y[:, 6 * w:7 * w])
    base = hc - (CONV_C_TAPS - 1) * sh
    cv = ccw_ref[0:1, :] * cbuf[base:base + tm, :]
    for j in range(1, CONV_C_TAPS):
        cv = cv + ccw_ref[j:j + 1, :] * cbuf[base + j * sh:base + j * sh + tm, :]
    oc_ref[0] = jax.nn.silu(_layernorm(cv, lncg_ref[...], lncb_ref[...]))
    nc_ref[0] = cbuf[hc + tm - (CONV_C_TAPS - 1) * sh:hc + tm, :]
    cbuf[0:hc, :] = cbuf[tm:tm + hc, :]

    q = y[:, 7 * w:8 * w]
    k = y[:, 8 * w:9 * w]
    v = y[:, 9 * w:10 * w]
    q_ref[0] = q
    k_ref[0] = k
    v_ref[0] = v
    kb_ref[0] = k.astype(BF16)
    vb_ref[0] = v.astype(BF16)


def _mixer(x, g_mix, w_mix, ln_v_g, ln_v_b, wsp, bsp, conv_b_w, conv_c_w, ln_c_g, ln_c_b,
           st_b, st_c, *, sample, tm, sh):
    B, S, D = x.shape
    w = BR_WIDTH
    hb, hc = st_b.shape[1], st_c.shape[1]
    nt = S // tm
    tok = lambda width: pl.BlockSpec((1, tm, width), lambda b, i: (b, i, 0))
    per_seq = lambda rows, width: pl.BlockSpec((1, rows, width), lambda b, i: (b, 0, 0))
    row = lambda a: a.reshape(1, -1)
    consts = [row(g_mix), w_mix, row(ln_v_g), row(ln_v_b), wsp, bsp, conv_b_w, conv_c_w,
              row(ln_c_g), row(ln_c_b)]
    f32_tok = jax.ShapeDtypeStruct((B, S, w), F32)
    bf_tok = jax.ShapeDtypeStruct((B, S, w), BF16)
    nb_rows, nc_rows = (CONV_B_TAPS - 1) * sh, (CONV_C_TAPS - 1) * sh
    return pl.pallas_call(
        functools.partial(_mixer_body, sample=sample, tm=tm, sh=sh, hb=hb, hc=hc),
        grid=(B, nt),
        in_specs=[tok(D)] + [_const_spec(c.shape) for c in consts]
                 + [per_seq(hb, w), per_seq(hc, w)],
        out_specs=[tok(w)] * 9 + [per_seq(nb_rows, w), per_seq(nc_rows, w)],
        out_shape=[f32_tok] * 6 + [bf_tok] * 2 + [f32_tok]
                  + [jax.ShapeDtypeStruct((B, nb_rows, w), F32),
                     jax.ShapeDtypeStruct((B, nc_rows, w), F32)],
        scratch_shapes=[pltpu.VMEM((hb + tm, w), F32), pltpu.VMEM((hc + tm, w), F32)],
        compiler_params=pltpu.CompilerParams(
            dimension_semantics=("arbitrary", "arbitrary"), vmem_limit_bytes=VMEM_LIMIT),
        name="mixer_sample" if sample else "mixer_prompt",
    )(x, *consts, st_b, st_c)


def _suffix_matrix():
    n = SB_KEY_BLOCK
    r = lax.broadcasted_iota(jnp.int32, (n, n), 0)
    c = lax.broadcasted_iota(jnp.int32, (n, n), 1)
    return (r > c).astype(BF16)


_NT = (((1,), (1,)), ((), ()))


def _neg_abs(x):
    return pltpu.bitcast(pltpu.bitcast(x, jnp.uint32) | jnp.uint32(0x80000000), F32)


def _sb_tile(qm, kt, vt, bias_col, run, suffix, mask, *, keys_in_lanes):
    if keys_in_lanes:
        z = _bdot(qm, kt) + bias_col
    else:
        z = lax.dot_general(qm, kt, _NT, preferred_element_type=F32) + bias_col
    sp = jnp.maximum(z, 0.0) + jnp.log(1.0 + jnp.exp(_neg_abs(z)))
    if mask is not None:
        sp = jnp.where(mask, sp, 0.0)
    nblk = z.shape[1] // SB_KEY_BLOCK
    later = [None] * nblk
    for b in reversed(range(nblk)):
        spb = sp[:, b * SB_KEY_BLOCK:(b + 1) * SB_KEY_BLOCK]
        later[b] = _bdot(spb.astype(BF16), suffix) + run
        run = later[b][:, 0:1] + spb[:, 0:1]
    later = later[0] if nblk == 1 else jnp.concatenate(later, axis=1)
    a = jnp.exp(z - sp - later)
    if mask is not None:
        a = jnp.where(mask, a, 0.0)
    a = a.astype(BF16)
    if keys_in_lanes:
        return lax.dot_general(a, vt, _NT, preferred_element_type=F32), run
    return _bdot(a, vt), run


def _head_stack(q, qm_ref, rows):
    lane_head = lax.broadcasted_iota(jnp.int32, q.shape, 1) // HEAD_DIM
    for h in range(N_HEADS):
        qm_ref[h * rows:(h + 1) * rows, :] = jnp.where(lane_head == h, q, 0.0).astype(BF16)


def _head_unstack(acc_ref, rows):
    lane_head = lax.broadcasted_iota(jnp.int32, (rows, N_HEADS * HEAD_DIM), 1) // HEAD_DIM
    out = jnp.where(lane_head == 0, acc_ref[0:rows, :], 0.0)
    for h in range(1, N_HEADS):
        out = out + jnp.where(lane_head == h, acc_ref[h * rows:(h + 1) * rows, :], 0.0)
    return out


def _attn_prompt_body(q_ref, kb_ref, vb_ref, bias_ref, o_ref, qm_ref, acc_ref, run_ref, *, tq, wide):
    i = pl.program_id(1)
    m = N_HEADS * tq
    blk = SB_KEY_BLOCK
    _head_stack(q_ref[0] * (HEAD_DIM ** -0.5), qm_ref, tq)
    qm = qm_ref[...]
    bias_col = bias_ref[...]
    suffix = _suffix_matrix()

    def keys(start, n):
        start = pl.multiple_of(start, blk)
        return kb_ref[0, pl.ds(start, n), :], vb_ref[0, pl.ds(start, n), :]

    jd = (i * tq) // blk
    qpos = i * tq + (lax.broadcasted_iota(jnp.int32, (m, blk), 0) & (tq - 1))
    kpos = jd * blk + lax.broadcasted_iota(jnp.int32, (m, blk), 1)
    pv, run = _sb_tile(qm, *keys(jd * blk, blk), bias_col, jnp.zeros((m, 1), F32), suffix,
                       kpos < qpos, keys_in_lanes=False)
    acc_ref[...] = pv
    run_ref[...] = run

    def step(start, n):
        pv, run = _sb_tile(qm, *keys(start, n), bias_col, run_ref[...], suffix, None,
                           keys_in_lanes=False)
        acc_ref[...] += pv
        run_ref[...] = run

    per_wide = wide // blk
    n_single = jd % per_wide
    n_wide = jd // per_wide

    def single_body(n, carry):
        step((jd - 1 - n) * blk, blk)
        return carry

    def wide_body(n, carry):
        step((n_wide - 1 - n) * wide, wide)
        return carry

    lax.fori_loop(0, n_single, single_body, 0)
    lax.fori_loop(0, n_wide, wide_body, 0)
    o_ref[0] = _head_unstack(acc_ref, tq)


def _attn_prompt(q, kb, vb, sb_bias, *, tq=SB_KEY_BLOCK, wide=4 * SB_KEY_BLOCK):
    B, S, w = q.shape
    assert tq <= SB_KEY_BLOCK and SB_KEY_BLOCK % tq == 0
    m = N_HEADS * tq
    bias_rows = jnp.repeat(sb_bias.astype(F32), tq).reshape(m, 1)
    return pl.pallas_call(
        functools.partial(_attn_prompt_body, tq=tq, wide=wide),
        grid=(B, S // tq),
        in_specs=[pl.BlockSpec((1, tq, w), lambda b, i: (b, i, 0)),
                  pl.BlockSpec((1, S, w), lambda b, i: (b, 0, 0)),
                  pl.BlockSpec((1, S, w), lambda b, i: (b, 0, 0)),
                  _const_spec((m, 1))],
        out_specs=pl.BlockSpec((1, tq, w), lambda b, i: (b, i, 0)),
        out_shape=jax.ShapeDtypeStruct((B, S, w), F32),
        scratch_shapes=[pltpu.VMEM((m, w), BF16), pltpu.VMEM((m, w), F32), pltpu.VMEM((m, 1), F32)],
        compiler_params=pltpu.CompilerParams(
            dimension_semantics=("arbitrary", "arbitrary"), vmem_limit_bytes=VMEM_LIMIT),
        name="attend_prompt",
    )(q, kb, vb, bias_rows)


def _attn_sample_body(pt_ref, q_ref, kn_ref, vn_ref, bias_ref, *refs, tpad, page, npg):
    k_pages = refs[0:npg]
    v_pages = refs[npg:2 * npg]
    o_ref, qm_ref, kt_ref, vt_ref, acc_ref, run_ref = refs[2 * npg:]
    g = pl.program_id(1)
    m = N_HEADS * tpad
    bias_col = bias_ref[...]
    suffix = _suffix_matrix()

    @pl.when(g == 0)
    def _():
        _head_stack(q_ref[0] * (HEAD_DIM ** -0.5), qm_ref, tpad)
        nk = SB_KEY_BLOCK
        kn = jnp.concatenate([kn_ref[0], jnp.zeros((nk - tpad, kn_ref.shape[2]), F32)], axis=0)
        vn = jnp.concatenate([vn_ref[0], jnp.zeros((nk - tpad, vn_ref.shape[2]), F32)], axis=0)
        t_q = lax.broadcasted_iota(jnp.int32, (m, nk), 0) & (tpad - 1)
        t_k = lax.broadcasted_iota(jnp.int32, (m, nk), 1)
        pv, run = _sb_tile(qm_ref[...], kn.astype(BF16), vn.astype(BF16), bias_col,
                           jnp.zeros((m, 1), F32), suffix, t_k < t_q, keys_in_lanes=False)
        acc_ref[...] = pv
        run_ref[...] = run

    @pl.when(g > 0)
    def _():
        for p in range(npg):
            kt_ref[:, p * page:(p + 1) * page] = k_pages[p][0, 0].astype(BF16)
            vt_ref[:, p * page:(p + 1) * page] = v_pages[p][0, 0].astype(BF16)
        pv, run = _sb_tile(qm_ref[...], kt_ref[...], vt_ref[...], bias_col, run_ref[...], suffix,
                           None, keys_in_lanes=True)
        acc_ref[...] += pv
        run_ref[...] = run

    @pl.when(g == pl.num_programs(1) - 1)
    def _():
        o_ref[0] = _head_unstack(acc_ref, tpad)


def _attn_sample(q, k_new, v_new, sb_bias, cache_kt, cache_vt, layer, page_table):
    n_seq, tpad, w = q.shape
    page = cache_kt.shape[3]
    n_pages = page_table.shape[1]
    npg = PAGES_PER_STEP
    n_groups = n_pages // npg
    m = N_HEADS * tpad
    ck, cv = cache_kt, cache_vt
    bias_rows = jnp.repeat(sb_bias.astype(F32), tpad).reshape(m, 1)

    def page_spec(p):
        def index(s, g, pt):
            grp = n_groups - jnp.maximum(g, 1)
            return (layer, pt[s * n_pages + grp * npg + p], 0, 0)
        return pl.BlockSpec((1, 1, w, page), index)

    per_seq = pl.BlockSpec((1, tpad, w), lambda s, g, pt: (s, 0, 0))
    grid_spec = pltpu.PrefetchScalarGridSpec(
        num_scalar_prefetch=1,
        grid=(n_seq, n_groups + 1),
        in_specs=[per_seq, per_seq, per_seq, pl.BlockSpec((m, 1), lambda s, g, pt: (0, 0))]
                 + [page_spec(p) for p in range(npg)] * 2,
        out_specs=per_seq,
        scratch_shapes=[pltpu.VMEM((m, w), BF16), pltpu.VMEM((w, npg * page), BF16),
                        pltpu.VMEM((w, npg * page), BF16), pltpu.VMEM((m, w), F32),
                        pltpu.VMEM((m, 1), F32)],
    )
    return pl.pallas_call(
        functools.partial(_attn_sample_body, tpad=tpad, page=page, npg=npg),
        grid_spec=grid_spec,
        out_shape=jax.ShapeDtypeStruct((n_seq, tpad, w), F32),
        compiler_params=pltpu.CompilerParams(
            dimension_semantics=("arbitrary", "arbitrary"), vmem_limit_bytes=VMEM_LIMIT),
        name="attend_sample",
    )(page_table.reshape(-1), q, k_new, v_new, bias_rows, *([ck] * npg), *([cv] * npg))


def _merge_body(x_ref, oa_ref, ob_ref, oc_ref, od_ref, gmix_ref, wgate_ref, wbr_ref, wo_ref, xo_ref):
    x = x_ref[...]
    d = x.shape[1]
    h = _rms(x, gmix_ref[...]).astype(BF16)
    merged = None
    for kk, br_ref in enumerate((oa_ref, ob_ref, oc_ref, od_ref)):
        gate = jax.nn.sigmoid(_bdot(h, wgate_ref[:, kk * d:(kk + 1) * d]))
        term = gate * _bdot(br_ref[...].astype(BF16), wbr_ref[kk])
        merged = term if merged is None else merged + term
    xo_ref[...] = x + _bdot(merged.astype(BF16), wo_ref[...])


def _merge(x, oa, ob, oc, od, g_mix, w_gate, w_br, w_o, *, tm):
    n, d = x.shape
    w = BR_WIDTH
    tok = lambda width: pl.BlockSpec((tm, width), lambda i: (i, 0))
    consts = [g_mix.reshape(1, -1), w_gate, w_br, w_o]
    return pl.pallas_call(
        _merge_body,
        grid=(n // tm,),
        in_specs=[tok(d)] + [tok(w)] * 4 + [_const_spec(c.shape) for c in consts],
        out_specs=tok(d),
        out_shape=jax.ShapeDtypeStruct((n, d), F32),
        compiler_params=pltpu.CompilerParams(
            dimension_semantics=("arbitrary",), vmem_limit_bytes=VMEM_LIMIT),
        name="merge",
    )(x, oa, ob, oc, od, *consts)


def _ffn_body(x_ref, p_ref, st_ref, gffn_ref, wup_ref, cw_ref, wdn_ref, gple_ref, wpg_ref,
              wple_ref, gfin_ref, xo_ref, nf_ref, ubuf, act_ref, *, tm, sh, hh, final):
    i = pl.program_id(1)

    @pl.when(i == 0)
    def _():
        ubuf[0:hh, :] = st_ref[0]

    x = x_ref[0]
    hn = _rms(x, gffn_ref[...]).astype(BF16)
    d_ff = wdn_ref.shape[0]
    cw = FFN_CHUNK
    ubuf[hh:hh + tm, :] = _bdot(hn, wup_ref[...])
    base = hh - (FFN_TAPS - 1) * sh
    for c in range(d_ff // cw):
        conv = []
        for col in (c * cw, d_ff + c * cw):
            cv = cw_ref[0:1, col:col + cw] * ubuf[base:base + tm, col:col + cw]
            for j in range(1, FFN_TAPS):
                cv = cv + cw_ref[j:j + 1, col:col + cw] * ubuf[base + j * sh:base + j * sh + tm, col:col + cw]
            conv.append(cv)
        act_ref[:, c * cw:(c + 1) * cw] = (jax.nn.silu(conv[0]) * conv[1]).astype(BF16)
    nf_ref[0] = ubuf[hh + tm - (FFN_TAPS - 1) * sh:hh + tm, :]
    ubuf[0:hh, :] = ubuf[tm:tm + hh, :]
    x = x + _bdot(act_ref[...], wdn_ref[...])
    gate = jax.nn.sigmoid(_bdot(_rms(x, gple_ref[...]).astype(BF16), wpg_ref[...]))
    x = x + gate * _bdot(p_ref[0].astype(BF16), wple_ref[...])
    xo_ref[0] = _rms(x, gfin_ref[...]) if final else x


def _ffn(x, p, st_f, g_ffn, w_up, ffn_conv_w, w_down, g_ple, w_ple_gate, w_ple, g_final,
         *, tm, sh, final):
    B, S, D = x.shape
    hh, up_cols = st_f.shape[1], st_f.shape[2]
    tok = lambda width: pl.BlockSpec((1, tm, width), lambda b, i: (b, i, 0))
    row = lambda a: a.reshape(1, -1)
    consts = [row(g_ffn), w_up, ffn_conv_w, w_down, row(g_ple), w_ple_gate, w_ple, row(g_final)]
    nf_rows = (FFN_TAPS - 1) * sh
    return pl.pallas_call(
        functools.partial(_ffn_body, tm=tm, sh=sh, hh=hh, final=final),
        grid=(B, S // tm),
        in_specs=[tok(D), tok(p.shape[2]), pl.BlockSpec((1, hh, up_cols), lambda b, i: (b, 0, 0))]
                 + [_const_spec(c.shape) for c in consts],
        out_specs=[tok(D), pl.BlockSpec((1, nf_rows, up_cols), lambda b, i: (b, 0, 0))],
        out_shape=[jax.ShapeDtypeStruct((B, S, D), F32),
                   jax.ShapeDtypeStruct((B, nf_rows, up_cols), F32)],
        scratch_shapes=[pltpu.VMEM((hh + tm, up_cols), F32), pltpu.VMEM((tm, up_cols // 2), BF16)],
        compiler_params=pltpu.CompilerParams(
            dimension_semantics=("arbitrary", "arbitrary"), vmem_limit_bytes=VMEM_LIMIT),
        name="ffn",
    )(x, p, st_f, *consts)


def _token_major(a):
    n_seq, t, c = a.shape
    return a.transpose(1, 0, 2).reshape(1, t * n_seq, c)


def _seq_major(a, n_seq):
    c = a.shape[-1]
    return a.reshape(-1, n_seq, c).transpose(1, 0, 2)


def kernel(x_prompt, x_sample, p_prompt, p_sample, cache_k, cache_v, page_table, state_conv_b, state_conv_c, state_ffn_conv, g_mix, w_in, ln_v_g, ln_v_b, w_sp, b_sp, conv_b_w, conv_c_w, ln_c_g, ln_c_b, sb_bias, w_br, w_o, g_ffn, w_up, ffn_conv_w, w_down, g_ple, w_ple_gate, w_ple, g_final):
    depth = w_in.shape[0]
    B, S, D = x_prompt.shape
    n_seq, T, _ = x_sample.shape
    w = BR_WIDTH
    n_grp = w_sp.shape[1]
    chunk = w_sp.shape[2]
    tm = min(512, S)
    tpad = 8
    assert S % tm == 0 and tm % chunk == 0 and S % SB_KEY_BLOCK == 0 and T <= tpad

    xp = x_prompt
    xs = _token_major(x_sample)
    zeros_b = jnp.zeros((B, 8, w), F32)
    zeros_c = jnp.zeros((B, 32, w), F32)
    zeros_f = jnp.zeros((B, 8, w_up.shape[2]), F32)
    n_phys, page = cache_k.shape[1], cache_k.shape[2]
    cache_kt = cache_k.transpose(0, 1, 3, 4, 2).reshape(depth, n_phys, w, page)
    cache_vt = cache_v.transpose(0, 1, 3, 4, 2).reshape(depth, n_phys, w, page)

    outs = {name: [] for name in ("kp", "vp", "cbp", "ccp", "cfp", "ks", "vs", "cvs", "cbs", "ccs", "cfs")}
    for l in range(depth):
        final = l == depth - 1
        w_mix = w_in[l, :, :MIX_COLS].astype(BF16)
        w_gate = w_in[l, :, MIX_COLS:].astype(BF16)
        w_br_l = w_br[l].astype(BF16)
        w_o_l = w_o[l].astype(BF16)
        w_up_l = w_up[l].astype(BF16)
        w_down_l = w_down[l].astype(BF16)
        w_pg_l = w_ple_gate[l].astype(BF16)
        w_ple_l = w_ple[l].astype(BF16)
        grp_cols = w // n_grp

        bmat = jnp.repeat(b_sp[l][:, :chunk].T, grp_cols, axis=1)
        (oa, ob, oc, q, k, v, kb, vb, _, nb, nc) = _mixer(
            xp, g_mix[l], w_mix, ln_v_g[l], ln_v_b[l], w_sp[l], bmat, conv_b_w[l], conv_c_w[l],
            ln_c_g[l], ln_c_b[l], zeros_b, zeros_c, sample=False, tm=tm, sh=1)
        od = _attn_prompt(q, kb, vb, sb_bias[l])
        flat = lambda a: a.reshape(B * S, a.shape[-1])
        x1 = _merge(flat(xp), flat(oa), flat(ob), flat(oc), flat(od), g_mix[l], w_gate, w_br_l,
                    w_o_l, tm=tm).reshape(B, S, D)
        xp, nf = _ffn(x1, p_prompt[l], zeros_f, g_ffn[l], w_up_l, ffn_conv_w[l], w_down_l,
                      g_ple[l], w_pg_l, w_ple_l, g_final, tm=tm, sh=1, final=final)
        outs["kp"].append(k.reshape(B, S, N_HEADS, HEAD_DIM))
        outs["vp"].append(v.reshape(B, S, N_HEADS, HEAD_DIM))
        outs["cbp"].append(nb)
        outs["ccp"].append(nc)
        outs["cfp"].append(nf)

        wv = jnp.repeat(w_sp[l][:, :T, :T].transpose(1, 2, 0).reshape(T * T, n_grp), grp_cols, axis=1)
        bv = jnp.repeat(b_sp[l][:, :T].T, grp_cols, axis=1)
        (oa, ob, oc, q, k, v, _, _, va, nb, nc) = _mixer(
            xs, g_mix[l], w_mix, ln_v_g[l], ln_v_b[l], wv, bv, conv_b_w[l], conv_c_w[l],
            ln_c_g[l], ln_c_b[l], _token_major(state_conv_b[l]), _token_major(state_conv_c[l]),
            sample=True, tm=T * n_seq, sh=n_seq)
        pad_t = lambda a: jnp.pad(_seq_major(a, n_seq), ((0, 0), (0, tpad - T), (0, 0)))
        od = _attn_sample(pad_t(q), pad_t(k), pad_t(v), sb_bias[l], cache_kt, cache_vt, l, page_table)
        od = _token_major(od[:, :T, :])
        x1 = _merge(xs[0], oa[0], ob[0], oc[0], od[0], g_mix[l], w_gate, w_br_l, w_o_l,
                    tm=T * n_seq)[None]
        xs, nf = _ffn(x1, _token_major(p_sample[l]), _token_major(state_ffn_conv[l]), g_ffn[l],
                      w_up_l, ffn_conv_w[l], w_down_l, g_ple[l], w_pg_l, w_ple_l, g_final,
                      tm=T * n_seq, sh=n_seq, final=final)
        outs["ks"].append(_seq_major(k, n_seq).reshape(n_seq, T, N_HEADS, HEAD_DIM))
        outs["vs"].append(_seq_major(v, n_seq).reshape(n_seq, T, N_HEADS, HEAD_DIM))
        outs["cvs"].append(_seq_major(va, n_seq))
        outs["cbs"].append(_seq_major(nb, n_seq))
        outs["ccs"].append(_seq_major(nc, n_seq))
        outs["cfs"].append(_seq_major(nf, n_seq))

    st = lambda name: jnp.stack(outs[name])
    return (xp, _seq_major(xs, n_seq), st("kp"), st("vp"), st("cbp"), st("ccp"), st("cfp"),
            st("ks"), st("vs"), st("cvs"), st("cbs"), st("ccs"), st("cfs"))
```

```python
import functools

import jax
import jax.numpy as jnp
from jax import lax
from jax.experimental import pallas as pl
from jax.experimental.pallas import tpu as pltpu

F32 = jnp.float32
BF16 = jnp.bfloat16

EPS = 1e-6
BR_WIDTH = 256
N_HEADS = 4
HEAD_DIM = 64
MIX_COLS = 10 * BR_WIDTH
CONV_B_TAPS = 3
CONV_C_TAPS = 31
FFN_TAPS = 3
FFN_CHUNK = 256
SB_KEY_BLOCK = 256
PAGES_PER_STEP = 16
SEQS_PER_STEP = 2
VMEM_LIMIT = 56 * 1024 * 1024


def _rms(x, g):
    return x * lax.rsqrt(jnp.mean(x * x, axis=-1, keepdims=True) + EPS) * g


def _layernorm(x, g, b):
    mu = jnp.mean(x, axis=-1, keepdims=True)
    xc = x - mu
    var = jnp.mean(xc * xc, axis=-1, keepdims=True)
    return xc * lax.rsqrt(var + EPS) * g + b


def _bdot(a, b):
    return jnp.dot(a, b, preferred_element_type=F32)


def _const_spec(shape):
    nd = len(shape)
    return pl.BlockSpec(shape, lambda *_: (0,) * nd, pipeline_mode=pl.Buffered(1))


def _whole(a):
    return a, _const_spec(a.shape)


def _layer_param(a, layer, cols=None):
    if a.ndim == 2:
        a = a[:, None, :]
    if cols is None:
        block = (None,) + a.shape[1:]
        index = (layer,) + (0,) * (a.ndim - 1)
    else:
        block = (pl.Element(1),) + tuple(pl.Element(n) for n in a.shape[1:-1]) + (pl.Element(cols[1]),)
        index = (layer,) + (0,) * (a.ndim - 2) + (cols[0],)
    return a, pl.BlockSpec(block, lambda *_: index, pipeline_mode=pl.Buffered(1))


def _mixer_body(x_ref, gmix_ref, wmix_ref, lnvg_ref, lnvb_ref, wsp_ref, bsp_ref,
                cbw_ref, ccw_ref, lncg_ref, lncb_ref, stb_ref, stc_ref,
                oa_ref, ob_ref, oc_ref, q_ref, k_ref, v_ref, kb_ref, vb_ref,
                va_ref, nb_ref, nc_ref, bbuf, cbuf, *, sample, tm, sh, hb, hc):
    i = pl.program_id(1)

    @pl.when(i == 0)
    def _():
        bbuf[0:hb, :] = stb_ref[0]
        cbuf[0:hc, :] = stc_ref[0]

    w = BR_WIDTH
    h = _rms(x_ref[0], gmix_ref[...]).astype(BF16)
    y = _bdot(h, wmix_ref[0])

    ga = jax.nn.gelu(y[:, 0:2 * w])
    u = ga[:, 0:w]
    va = _layernorm(ga[:, w:2 * w], lnvg_ref[...], lnvb_ref[...])
    va_ref[0] = va
    parts = []
    if sample:
        nt = tm // sh
        for t in range(nt):
            s = jnp.broadcast_to(bsp_ref[t:t + 1, :], (sh, w))
            for s2 in range(t + 1):
                s = s + wsp_ref[t * nt + s2:t * nt + s2 + 1, :] * va[s2 * sh:(s2 + 1) * sh, :]
            parts.append(s)
    else:
        ch = wsp_ref.shape[1]
        rr = lax.broadcasted_iota(jnp.int32, (ch, ch), 0)
        cc = lax.broadcasted_iota(jnp.int32, (ch, ch), 1)
        grp = lax.broadcasted_iota(jnp.int32, (ch, w), 1) // (w // wsp_ref.shape[0])
        wg = [jnp.where(cc <= rr, wsp_ref[g], 0.0).astype(BF16) for g in range(wsp_ref.shape[0])]
        for c in range(tm // ch):
            vc = va[c * ch:(c + 1) * ch, :]
            s = bsp_ref[...]
            for g in range(wsp_ref.shape[0]):
                s = s + _bdot(wg[g], jnp.where(grp == g, vc, 0.0).astype(BF16))
            parts.append(s)
    oa_ref[0] = u * jnp.concatenate(parts, axis=0)

    bbuf[hb:hb + tm, :] = y[:, 3 * w:4 * w] * y[:, 4 * w:5 * w]
    base = hb - (CONV_B_TAPS - 1) * sh
    cb = cbw_ref[0:1, :] * bbuf[base:base + tm, :]
    for j in range(1, CONV_B_TAPS):
        cb = cb + cbw_ref[j:j + 1, :] * bbuf[base + j * sh:base + j * sh + tm, :]
    ob_ref[0] = y[:, 2 * w:3 * w] * cb
    nb_ref[0] = bbuf[hb + tm - (CONV_B_TAPS - 1) * sh:hb + tm, :]
    bbuf[0:hb, :] = bbuf[tm:tm + hb, :]

    cbuf[hc:hc + tm, :] = y[:, 5 * w:6 * w] * jax.nn.sigmoid(y[:, 6 * w:7 * w])
    base = hc - (CONV_C_TAPS - 1) * sh
    cv = ccw_ref[0:1, :] * cbuf[base:base + tm, :]
    for j in range(1, CONV_C_TAPS):
        cv = cv + ccw_ref[j:j + 1, :] * cbuf[base + j * sh:base + j * sh + tm, :]
    oc_ref[0] = jax.nn.silu(_layernorm(cv, lncg_ref[...], lncb_ref[...]))
    nc_ref[0] = cbuf[hc + tm - (CONV_C_TAPS - 1) * sh:hc + tm, :]
    cbuf[0:hc, :] = cbuf[tm:tm + hc, :]

    q = y[:, 7 * w:8 * w]
    k = y[:, 8 * w:9 * w]
    v = y[:, 9 * w:10 * w]
    q_ref[0] = q
    k_ref[0] = k
    v_ref[0] = v
    kb_ref[0] = k.astype(BF16)
    vb_ref[0] = v.astype(BF16)


def _mixer(x, params, st_b, st_c, *, sample, tm, sh):
    B, S, D = x.shape
    w = BR_WIDTH
    hb, hc = st_b.shape[1], st_c.shape[1]
    nt = S // tm
    tok = lambda width: pl.BlockSpec((1, tm, width), lambda b, i: (b, i, 0))
    per_seq = lambda rows, width: pl.BlockSpec((1, rows, width), lambda b, i: (b, 0, 0))
    consts = [a for a, _ in params]
    f32_tok = jax.ShapeDtypeStruct((B, S, w), F32)
    bf_tok = jax.ShapeDtypeStruct((B, S, w), BF16)
    nb_rows, nc_rows = (CONV_B_TAPS - 1) * sh, (CONV_C_TAPS - 1) * sh
    return pl.pallas_call(
        functools.partial(_mixer_body, sample=sample, tm=tm, sh=sh, hb=hb, hc=hc),
        grid=(B, nt),
        in_specs=[tok(D)] + [spec for _, spec in params] + [per_seq(hb, w), per_seq(hc, w)],
        out_specs=[tok(w)] * 9 + [per_seq(nb_rows, w), per_seq(nc_rows, w)],
        out_shape=[f32_tok] * 6 + [bf_tok] * 2 + [f32_tok]
                  + [jax.ShapeDtypeStruct((B, nb_rows, w), F32),
                     jax.ShapeDtypeStruct((B, nc_rows, w), F32)],
        scratch_shapes=[pltpu.VMEM((hb + tm, w), F32), pltpu.VMEM((hc + tm, w), F32)],
        compiler_params=pltpu.CompilerParams(
            dimension_semantics=("arbitrary", "arbitrary"), vmem_limit_bytes=VMEM_LIMIT),
        name="mixer_sample" if sample else "mixer_prompt",
    )(x, *consts, st_b, st_c)


def _suffix_matrix():
    n = SB_KEY_BLOCK
    r = lax.broadcasted_iota(jnp.int32, (n, n), 0)
    c = lax.broadcasted_iota(jnp.int32, (n, n), 1)
    return (r > c).astype(BF16)


_NT = (((1,), (1,)), ((), ()))


def _neg_abs(x):
    return pltpu.bitcast(pltpu.bitcast(x, jnp.uint32) | jnp.uint32(0x80000000), F32)


def _sb_tile(qm, kt, vt, bias_col, run, suffix, mask, *, keys_in_lanes, stack_blocks=False):
    if keys_in_lanes:
        z = _bdot(qm.astype(kt.dtype), kt) + bias_col
    else:
        z = lax.dot_general(qm.astype(kt.dtype), kt, _NT, preferred_element_type=F32) + bias_col
    sp = jnp.maximum(z, 0.0) + jnp.log(1.0 + jnp.exp(_neg_abs(z)))
    if mask is not None:
        sp = jnp.where(mask, sp, 0.0)
    m = z.shape[0]
    nblk = z.shape[1] // SB_KEY_BLOCK
    blocks = [sp[:, b * SB_KEY_BLOCK:(b + 1) * SB_KEY_BLOCK] for b in range(nblk)]
    if stack_blocks and nblk > 1:
        stacked = _bdot(jnp.concatenate(blocks, axis=0).astype(BF16), suffix)
        within = [stacked[b * m:(b + 1) * m, :] for b in range(nblk)]
    else:
        within = [_bdot(blk.astype(BF16), suffix) for blk in blocks]
    later = [None] * nblk
    for b in reversed(range(nblk)):
        later[b] = within[b] + run
        run = run + (within[b][:, 0:1] + blocks[b][:, 0:1])
    later = later[0] if nblk == 1 else jnp.concatenate(later, axis=1)
    a = jnp.exp(z - sp - later)
    if mask is not None:
        a = jnp.where(mask, a, 0.0)
    a = a.astype(BF16).astype(vt.dtype)
    if keys_in_lanes:
        return lax.dot_general(a, vt, _NT, preferred_element_type=F32), run
    return _bdot(a, vt), run


def _head_stack(q, qm_ref, rows):
    lane_head = lax.broadcasted_iota(jnp.int32, q.shape, 1) // HEAD_DIM
    for h in range(N_HEADS):
        qm_ref[h * rows:(h + 1) * rows, :] = jnp.where(lane_head == h, q, 0.0).astype(BF16)


def _head_unstack(acc_ref, rows):
    lane_head = lax.broadcasted_iota(jnp.int32, (rows, N_HEADS * HEAD_DIM), 1) // HEAD_DIM
    out = jnp.where(lane_head == 0, acc_ref[0:rows, :], 0.0)
    for h in range(1, N_HEADS):
        out = out + jnp.where(lane_head == h, acc_ref[h * rows:(h + 1) * rows, :], 0.0)
    return out


def _attn_prompt_body(q_ref, kb_ref, vb_ref, bias_ref, o_ref, qm_ref, acc_ref, run_ref, *, tq, wide):
    i = pl.program_id(1)
    m = N_HEADS * tq
    blk = SB_KEY_BLOCK
    _head_stack(q_ref[0] * (HEAD_DIM ** -0.5), qm_ref, tq)
    qm = qm_ref[...]
    bias_col = bias_ref[...]
    suffix = _suffix_matrix()

    def keys(start, n):
        start = pl.multiple_of(start, blk)
        return kb_ref[0, pl.ds(start, n), :], vb_ref[0, pl.ds(start, n), :]

    jd = (i * tq) // blk
    qpos = i * tq + (lax.broadcasted_iota(jnp.int32, (m, blk), 0) & (tq - 1))
    kpos = jd * blk + lax.broadcasted_iota(jnp.int32, (m, blk), 1)
    pv, run = _sb_tile(qm, *keys(jd * blk, blk), bias_col, jnp.zeros((m, 1), F32), suffix,
                       kpos < qpos, keys_in_lanes=False)
    acc_ref[...] = pv
    run_ref[...] = run

    def step(start, n):
        pv, run = _sb_tile(qm, *keys(start, n), bias_col, run_ref[...], suffix, None,
                           keys_in_lanes=False)
        acc_ref[...] += pv
        run_ref[...] = run

    per_wide = wide // blk
    n_single = jd % per_wide
    n_wide = jd // per_wide

    def single_body(n, carry):
        step((jd - 1 - n) * blk, blk)
        return carry

    def wide_body(n, carry):
        step((n_wide - 1 - n) * wide, wide)
        return carry

    lax.fori_loop(0, n_single, single_body, 0)
    lax.fori_loop(0, n_wide, wide_body, 0)
    o_ref[0] = _head_unstack(acc_ref, tq)


def _attn_prompt(q, kb, vb, sb_bias, *, tq=SB_KEY_BLOCK, wide=2 * SB_KEY_BLOCK):
    B, S, w = q.shape
    assert tq <= SB_KEY_BLOCK and SB_KEY_BLOCK % tq == 0
    m = N_HEADS * tq
    bias_rows = jnp.repeat(sb_bias.astype(F32), tq).reshape(m, 1)
    return pl.pallas_call(
        functools.partial(_attn_prompt_body, tq=tq, wide=wide),
        grid=(B, S // tq),
        in_specs=[pl.BlockSpec((1, tq, w), lambda b, i: (b, i, 0)),
                  pl.BlockSpec((1, S, w), lambda b, i: (b, 0, 0)),
                  pl.BlockSpec((1, S, w), lambda b, i: (b, 0, 0)),
                  _const_spec((m, 1))],
        out_specs=pl.BlockSpec((1, tq, w), lambda b, i: (b, i, 0)),
        out_shape=jax.ShapeDtypeStruct((B, S, w), F32),
        scratch_shapes=[pltpu.VMEM((m, w), BF16), pltpu.VMEM((m, w), F32), pltpu.VMEM((m, 1), F32)],
        compiler_params=pltpu.CompilerParams(
            dimension_semantics=("arbitrary", "arbitrary"), vmem_limit_bytes=VMEM_LIMIT),
        name="attend_prompt",
    )(q, kb, vb, bias_rows)


def _attn_sample_body(pt_ref, q_ref, kn_ref, vn_ref, bias_ref, *refs, tpad, page, npg, nsq):
    k_pages = refs[0:nsq * npg]
    v_pages = refs[nsq * npg:2 * nsq * npg]
    o_ref, qm_ref, acc_ref, run_ref = refs[2 * nsq * npg:]
    g = pl.program_id(1)
    m = N_HEADS * tpad
    bias_col = bias_ref[...]
    suffix = _suffix_matrix()

    @pl.when(g == 0)
    def _():
        nk = SB_KEY_BLOCK
        t_q = lax.broadcasted_iota(jnp.int32, (m, nk), 0) & (tpad - 1)
        t_k = lax.broadcasted_iota(jnp.int32, (m, nk), 1)
        for j in range(nsq):
            rows = slice(j * m, (j + 1) * m)
            _head_stack(q_ref[j] * (HEAD_DIM ** -0.5), qm_ref.at[rows], tpad)
            kn = jnp.concatenate([kn_ref[j], jnp.zeros((nk - tpad, kn_ref.shape[2]), F32)], axis=0)
            vn = jnp.concatenate([vn_ref[j], jnp.zeros((nk - tpad, vn_ref.shape[2]), F32)], axis=0)
            pv, run = _sb_tile(qm_ref[rows, :], kn.astype(BF16), vn.astype(BF16), bias_col,
                               jnp.zeros((m, 1), F32), suffix, t_k < t_q, keys_in_lanes=False)
            acc_ref[rows, :] = pv
            run_ref[rows, :] = run

    @pl.when(g > 0)
    def _():
        for j in range(nsq):
            rows = slice(j * m, (j + 1) * m)
            kt = jnp.concatenate([k_pages[j * npg + p][0, 0] for p in range(npg)], axis=1)
            vt = jnp.concatenate([v_pages[j * npg + p][0, 0] for p in range(npg)], axis=1)
            pv, run = _sb_tile(qm_ref[rows, :], kt, vt, bias_col, run_ref[rows, :], suffix, None,
                               keys_in_lanes=True, stack_blocks=True)
            acc_ref[rows, :] += pv
            run_ref[rows, :] = run

    @pl.when(g == pl.num_programs(1) - 1)
    def _():
        for j in range(nsq):
            o_ref[j] = _head_unstack(acc_ref.at[j * m:(j + 1) * m], tpad)


def _attn_sample(q, k_new, v_new, sb_bias, cache_kt, cache_vt, layer, page_table):
    n_seq, tpad, w = q.shape
    page = cache_kt.shape[3]
    n_pages = page_table.shape[1]
    npg = PAGES_PER_STEP
    nsq = SEQS_PER_STEP
    n_groups = n_pages // npg
    m = N_HEADS * tpad
    assert n_pages % npg == 0 and n_seq % nsq == 0
    bias_rows = jnp.repeat(sb_bias.astype(F32), tpad).reshape(m, 1)

    def page_spec(j, p):
        def index(s, g, pt):
            grp = n_groups - jnp.maximum(g, 1)
            return (layer, pt[(s * nsq + j) * n_pages + grp * npg + p], 0, 0)
        return pl.BlockSpec((1, 1, w, page), index)

    per_step = pl.BlockSpec((nsq, tpad, w), lambda s, g, pt: (s, 0, 0))
    page_specs = [page_spec(j, p) for j in range(nsq) for p in range(npg)]
    grid_spec = pltpu.PrefetchScalarGridSpec(
        num_scalar_prefetch=1,
        grid=(n_seq // nsq, n_groups + 1),
        in_specs=[per_step, per_step, per_step, pl.BlockSpec((m, 1), lambda s, g, pt: (0, 0))]
                 + page_specs * 2,
        out_specs=per_step,
        scratch_shapes=[pltpu.VMEM((nsq * m, w), BF16), pltpu.VMEM((nsq * m, w), F32),
                        pltpu.VMEM((nsq * m, 1), F32)],
    )
    n_in = nsq * npg
    return pl.pallas_call(
        functools.partial(_attn_sample_body, tpad=tpad, page=page, npg=npg, nsq=nsq),
        grid_spec=grid_spec,
        out_shape=jax.ShapeDtypeStruct((n_seq, tpad, w), F32),
        compiler_params=pltpu.CompilerParams(
            dimension_semantics=("arbitrary", "arbitrary"), vmem_limit_bytes=VMEM_LIMIT),
        name="attend_sample",
    )(page_table.reshape(-1), q, k_new, v_new, bias_rows, *([cache_kt] * n_in), *([cache_vt] * n_in))


def _merge_body(x_ref, oa_ref, ob_ref, oc_ref, od_ref, gmix_ref, wgate_ref, wbr_ref, wo_ref, xo_ref):
    x = x_ref[...]
    d = x.shape[1]
    h = _rms(x, gmix_ref[...]).astype(BF16)
    merged = None
    for kk, br_ref in enumerate((oa_ref, ob_ref, oc_ref, od_ref)):
        gate = jax.nn.sigmoid(_bdot(h, wgate_ref[0, :, kk * d:(kk + 1) * d]))
        term = gate * _bdot(br_ref[...].astype(BF16), wbr_ref[kk])
        merged = term if merged is None else merged + term
    xo_ref[...] = x + _bdot(merged.astype(BF16), wo_ref[...])


def _merge(x, oa, ob, oc, od, params, *, tm):
    n, d = x.shape
    w = BR_WIDTH
    tok = lambda width: pl.BlockSpec((tm, width), lambda i: (i, 0))
    consts = [a for a, _ in params]
    return pl.pallas_call(
        _merge_body,
        grid=(n // tm,),
        in_specs=[tok(d)] + [tok(w)] * 4 + [spec for _, spec in params],
        out_specs=tok(d),
        out_shape=jax.ShapeDtypeStruct((n, d), F32),
        compiler_params=pltpu.CompilerParams(
            dimension_semantics=("arbitrary",), vmem_limit_bytes=VMEM_LIMIT),
        name="merge",
    )(x, oa, ob, oc, od, *consts)


def _ffn_body(x_ref, p_ref, st_ref, gffn_ref, wup_ref, cw_ref, wdn_ref, gple_ref, wpg_ref,
              wple_ref, gfin_ref, xo_ref, nf_ref, ubuf, act_ref, *, tm, sh, hh, final):
    i = pl.program_id(1)

    @pl.when(i == 0)
    def _():
        ubuf[0:hh, :] = st_ref[0]

    x = x_ref[0]
    hn = _rms(x, gffn_ref[...]).astype(BF16)
    d_ff = wdn_ref.shape[0]
    cw = FFN_CHUNK
    ubuf[hh:hh + tm, :] = _bdot(hn, wup_ref[...])
    base = hh - (FFN_TAPS - 1) * sh
    for c in range(d_ff // cw):
        conv = []
        for col in (c * cw, d_ff + c * cw):
            cv = cw_ref[0:1, col:col + cw] * ubuf[base:base + tm, col:col + cw]
            for j in range(1, FFN_TAPS):
                cv = cv + cw_ref[j:j + 1, col:col + cw] * ubuf[base + j * sh:base + j * sh + tm, col:col + cw]
            conv.append(cv)
        act_ref[:, c * cw:(c + 1) * cw] = (jax.nn.silu(conv[0]) * conv[1]).astype(BF16)
    nf_ref[0] = ubuf[hh + tm - (FFN_TAPS - 1) * sh:hh + tm, :]
    ubuf[0:hh, :] = ubuf[tm:tm + hh, :]
    x = x + _bdot(act_ref[...], wdn_ref[...])
    gate = jax.nn.sigmoid(_bdot(_rms(x, gple_ref[...]).astype(BF16), wpg_ref[...]))
    x = x + gate * _bdot(p_ref[0].astype(BF16), wple_ref[...])
    xo_ref[0] = _rms(x, gfin_ref[...]) if final else x


def _ffn(x, p, st_f, params, *, tm, sh, final):
    B, S, D = x.shape
    hh, up_cols = st_f.shape[1], st_f.shape[2]
    tok = lambda width: pl.BlockSpec((1, tm, width), lambda b, i: (b, i, 0))
    consts = [a for a, _ in params]
    p, p_spec = p
    nf_rows = (FFN_TAPS - 1) * sh
    return pl.pallas_call(
        functools.partial(_ffn_body, tm=tm, sh=sh, hh=hh, final=final),
        grid=(B, S // tm),
        in_specs=[tok(D), p_spec, pl.BlockSpec((1, hh, up_cols), lambda b, i: (b, 0, 0))]
                 + [spec for _, spec in params],
        out_specs=[tok(D), pl.BlockSpec((1, nf_rows, up_cols), lambda b, i: (b, 0, 0))],
        out_shape=[jax.ShapeDtypeStruct((B, S, D), F32),
                   jax.ShapeDtypeStruct((B, nf_rows, up_cols), F32)],
        scratch_shapes=[pltpu.VMEM((hh + tm, up_cols), F32), pltpu.VMEM((tm, up_cols // 2), BF16)],
        compiler_params=pltpu.CompilerParams(
            dimension_semantics=("arbitrary", "arbitrary"), vmem_limit_bytes=VMEM_LIMIT),
        name="ffn",
    )(x, p, st_f, *consts)


def _token_major(a):
    n_seq, t, c = a.shape
    return a.transpose(1, 0, 2).reshape(1, t * n_seq, c)


def _seq_major(a, n_seq):
    c = a.shape[-1]
    return a.reshape(-1, n_seq, c).transpose(1, 0, 2)


def kernel(x_prompt, x_sample, p_prompt, p_sample, cache_k, cache_v, page_table, state_conv_b, state_conv_c, state_ffn_conv, g_mix, w_in, ln_v_g, ln_v_b, w_sp, b_sp, conv_b_w, conv_c_w, ln_c_g, ln_c_b, sb_bias, w_br, w_o, g_ffn, w_up, ffn_conv_w, w_down, g_ple, w_ple_gate, w_ple, g_final):
    depth = w_in.shape[0]
    B, S, D = x_prompt.shape
    n_seq, T, _ = x_sample.shape
    w = BR_WIDTH
    n_grp = w_sp.shape[1]
    chunk = w_sp.shape[2]
    tm = min(512, S)
    tpad = 8
    assert S % tm == 0 and tm % chunk == 0 and S % SB_KEY_BLOCK == 0 and T <= tpad

    xp = x_prompt
    xs = _token_major(x_sample)
    zeros_b = jnp.zeros((B, 8, w), F32)
    zeros_c = jnp.zeros((B, 32, w), F32)
    zeros_f = jnp.zeros((B, 8, w_up.shape[2]), F32)
    n_phys, page = cache_k.shape[1], cache_k.shape[2]
    cache_kt = cache_k.transpose(0, 1, 3, 4, 2).reshape(depth, n_phys, w, page)
    cache_vt = cache_v.transpose(0, 1, 3, 4, 2).reshape(depth, n_phys, w, page)

    w_in_b = w_in.astype(BF16)
    w_br_b = w_br.astype(BF16)
    w_o_b = w_o.astype(BF16)
    w_up_b = w_up.astype(BF16)
    w_down_b = w_down.astype(BF16)
    w_pg_b = w_ple_gate.astype(BF16)
    w_ple_b = w_ple.astype(BF16)
    grp_cols = w // n_grp
    ple_dim = p_prompt.shape[-1]

    outs = {name: [] for name in ("kp", "vp", "cbp", "ccp", "cfp", "ks", "vs", "cvs", "cbs", "ccs", "cfs")}
    for l in range(depth):
        final = l == depth - 1
        lp = functools.partial(_layer_param, layer=l)
        mixer_shared = [lp(g_mix), lp(w_in_b, cols=(0, MIX_COLS)), lp(ln_v_g), lp(ln_v_b)]
        mixer_conv = [lp(conv_b_w), lp(conv_c_w), lp(ln_c_g), lp(ln_c_b)]
        merge_params = [lp(g_mix), lp(w_in_b, cols=(MIX_COLS, w_in.shape[2] - MIX_COLS)),
                        lp(w_br_b), lp(w_o_b)]
        ffn_params = [lp(g_ffn), lp(w_up_b), lp(ffn_conv_w), lp(w_down_b), lp(g_ple), lp(w_pg_b),
                      lp(w_ple_b), _whole(g_final.reshape(1, -1))]

        bmat = jnp.repeat(b_sp[l][:, :chunk].T, grp_cols, axis=1)
        (oa, ob, oc, q, k, v, kb, vb, _, nb, nc) = _mixer(
            xp, mixer_shared + [lp(w_sp), _whole(bmat)] + mixer_conv, zeros_b, zeros_c,
            sample=False, tm=tm, sh=1)
        od = _attn_prompt(q, kb, vb, sb_bias[l])
        flat = lambda a: a.reshape(B * S, a.shape[-1])
        x1 = _merge(flat(xp), flat(oa), flat(ob), flat(oc), flat(od), merge_params,
                    tm=tm).reshape(B, S, D)
        p_spec = pl.BlockSpec((None, 1, tm, ple_dim), lambda b, i, l=l: (l, b, i, 0))
        xp, nf = _ffn(x1, (p_prompt, p_spec), zeros_f, ffn_params, tm=tm, sh=1, final=final)
        outs["kp"].append(k.reshape(B, S, N_HEADS, HEAD_DIM))
        outs["vp"].append(v.reshape(B, S, N_HEADS, HEAD_DIM))
        outs["cbp"].append(nb)
        outs["ccp"].append(nc)
        outs["cfp"].append(nf)

        wv = jnp.repeat(w_sp[l][:, :T, :T].transpose(1, 2, 0).reshape(T * T, n_grp), grp_cols, axis=1)
        bv = jnp.repeat(b_sp[l][:, :T].T, grp_cols, axis=1)
        (oa, ob, oc, q, k, v, _, _, va, nb, nc) = _mixer(
            xs, mixer_shared + [_whole(wv), _whole(bv)] + mixer_conv,
            _token_major(state_conv_b[l]), _token_major(state_conv_c[l]),
            sample=True, tm=T * n_seq, sh=n_seq)
        pad_t = lambda a: jnp.pad(_seq_major(a, n_seq), ((0, 0), (0, tpad - T), (0, 0)))
        od = _attn_sample(pad_t(q), pad_t(k), pad_t(v), sb_bias[l], cache_kt, cache_vt, l, page_table)
        od = _token_major(od[:, :T, :])
        x1 = _merge(xs[0], oa[0], ob[0], oc[0], od[0], merge_params, tm=T * n_seq)[None]
        ps = _token_major(p_sample[l])
        p_spec = pl.BlockSpec((1, T * n_seq, ple_dim), lambda b, i: (b, i, 0))
        xs, nf = _ffn(x1, (ps, p_spec), _token_major(state_ffn_conv[l]), ffn_params,
                      tm=T * n_seq, sh=n_seq, final=final)
        outs["ks"].append(_seq_major(k, n_seq).reshape(n_seq, T, N_HEADS, HEAD_DIM))
        outs["vs"].append(_seq_major(v, n_seq).reshape(n_seq, T, N_HEADS, HEAD_DIM))
        outs["cvs"].append(_seq_major(va, n_seq))
        outs["cbs"].append(_seq_major(nb, n_seq))
        outs["ccs"].append(_seq_major(nc, n_seq))
        outs["cfs"].append(_seq_major(nf, n_seq))

    st = lambda name: jnp.stack(outs[name])
    return (xp, _seq_major(xs, n_seq), st("kp"), st("vp"), st("cbp"), st("ccp"), st("cfp"),
            st("ks"), st("vs"), st("cvs"), st("cbs"), st("ccs"), st("cfs"))
```

```python
import functools

import jax
import jax.numpy as jnp
from jax import lax
from jax.experimental import pallas as pl
from jax.experimental.pallas import tpu as pltpu

F32 = jnp.float32
BF16 = jnp.bfloat16

EPS = 1e-6
SUBLANES = 8
BR_WIDTH = 256
N_HEADS = 4
HEAD_DIM = 64
MIX_COLS = 10 * BR_WIDTH
CONV_B_TAPS = 3
CONV_C_TAPS = 31
FFN_TAPS = 3
FFN_CHUNK = 256
SB_KEY_BLOCK = 256
PAGES_PER_STEP = 16
SEQS_PER_STEP = 2
VMEM_LIMIT = 56 * 1024 * 1024


def _rms(x, g):
    return x * lax.rsqrt(jnp.mean(x * x, axis=-1, keepdims=True) + EPS) * g


def _layernorm(x, g, b):
    mu = jnp.mean(x, axis=-1, keepdims=True)
    xc = x - mu
    var = jnp.mean(xc * xc, axis=-1, keepdims=True)
    return xc * lax.rsqrt(var + EPS) * g + b


def _bdot(a, b):
    return jnp.dot(a, b, preferred_element_type=F32)


def _const_spec(shape):
    nd = len(shape)
    return pl.BlockSpec(shape, lambda *_: (0,) * nd, pipeline_mode=pl.Buffered(1))


def _whole(a):
    return a, _const_spec(a.shape)


def _layer_param(a, layer, cols=None):
    if a.ndim == 2:
        a = a[:, None, :]
    if cols is None:
        block = (None,) + a.shape[1:]
        index = (layer,) + (0,) * (a.ndim - 1)
    else:
        block = (pl.Element(1),) + tuple(pl.Element(n) for n in a.shape[1:-1]) + (pl.Element(cols[1]),)
        index = (layer,) + (0,) * (a.ndim - 2) + (cols[0],)
    return a, pl.BlockSpec(block, lambda *_: index, pipeline_mode=pl.Buffered(1))


def _mixer_body(x_ref, gmix_ref, wmix_ref, lnvg_ref, lnvb_ref, wsp_ref, bsp_ref,
                cbw_ref, ccw_ref, lncg_ref, lncb_ref, stb_ref, stc_ref,
                oa_ref, ob_ref, oc_ref, q_ref, k_ref, v_ref, kb_ref, vb_ref,
                va_ref, nb_ref, nc_ref, bbuf, cbuf, wbuf, *, sample, tm, sh, hb, hc):
    i = pl.program_id(1)

    @pl.when(i == 0)
    def _():
        bbuf[0:hb, :] = stb_ref[0]
        cbuf[0:hc, :] = stc_ref[0]

    w = BR_WIDTH
    h = _rms(x_ref[0], gmix_ref[...]).astype(BF16)
    y = _bdot(h, wmix_ref[0])

    ga = jax.nn.gelu(y[:, 0:2 * w])
    u = ga[:, 0:w]
    va = _layernorm(ga[:, w:2 * w], lnvg_ref[...], lnvb_ref[...])
    va_ref[0] = va
    parts = []
    if sample:
        nt = tm // sh
        for t in range(nt):
            s = jnp.broadcast_to(bsp_ref[t:t + 1, :], (sh, w))
            for s2 in range(t + 1):
                s = s + wsp_ref[t * nt + s2:t * nt + s2 + 1, :] * va[s2 * sh:(s2 + 1) * sh, :]
            parts.append(s)
    else:
        ch = wsp_ref.shape[1]
        rr = lax.broadcasted_iota(jnp.int32, (ch, ch), 0)
        cc = lax.broadcasted_iota(jnp.int32, (ch, ch), 1)
        grp = lax.broadcasted_iota(jnp.int32, (ch, w), 1) // (w // wsp_ref.shape[0])
        wg = [jnp.where(cc <= rr, wsp_ref[g], 0.0).astype(BF16) for g in range(wsp_ref.shape[0])]
        for c in range(tm // ch):
            vc = va[c * ch:(c + 1) * ch, :]
            s = bsp_ref[...]
            for g in range(wsp_ref.shape[0]):
                s = s + _bdot(wg[g], jnp.where(grp == g, vc, 0.0).astype(BF16))
            parts.append(s)
    oa_ref[0] = u * jnp.concatenate(parts, axis=0)

    bbuf[hb:hb + tm, :] = y[:, 3 * w:4 * w] * y[:, 4 * w:5 * w]
    base = hb - (CONV_B_TAPS - 1) * sh
    cb = cbw_ref[0:1, :] * bbuf[base:base + tm, :]
    for j in range(1, CONV_B_TAPS):
        cb = cb + cbw_ref[j:j + 1, :] * bbuf[base + j * sh:base + j * sh + tm, :]
    ob_ref[0] = y[:, 2 * w:3 * w] * cb
    nb_ref[0] = bbuf[hb + tm - (CONV_B_TAPS - 1) * sh:hb + tm, :]
    bbuf[0:hb, :] = bbuf[tm:tm + hb, :]

    cbuf[hc:hc + tm, :] = y[:, 5 * w:6 * w] * jax.nn.sigmoid(y[:, 6 * w:7 * w])
    base = hc - (CONV_C_TAPS - 1) * sh
    cv = None
    residues = SUBLANES if sh % SUBLANES else 1
    for b in range(residues):
        taps = list(range(b, CONV_C_TAPS, residues))
        span = (taps[-1] - b) * sh
        start = base + b * sh
        if start % SUBLANES:
            win = wbuf.at[b]
            win[0:span + tm, :] = cbuf[start:start + span + tm, :]
            start = 0
        else:
            win = cbuf
        for j in taps:
            off = start + (j - b) * sh
            term = ccw_ref[j:j + 1, :] * win[off:off + tm, :]
            cv = term if cv is None else cv + term
    oc_ref[0] = jax.nn.silu(_layernorm(cv, lncg_ref[...], lncb_ref[...]))
    nc_ref[0] = cbuf[hc + tm - (CONV_C_TAPS - 1) * sh:hc + tm, :]
    cbuf[0:hc, :] = cbuf[tm:tm + hc, :]

    q = y[:, 7 * w:8 * w]
    k = y[:, 8 * w:9 * w]
    v = y[:, 9 * w:10 * w]
    q_ref[0] = q
    k_ref[0] = k if sample else k.T
    v_ref[0] = v if sample else v.T
    kb_ref[0] = k.astype(BF16)
    vb_ref[0] = v.astype(BF16)


def _mixer(x, params, st_b, st_c, *, sample, tm, sh):
    B, S, D = x.shape
    w = BR_WIDTH
    hb, hc = st_b.shape[1], st_c.shape[1]
    nt = S // tm
    tok = lambda width: pl.BlockSpec((1, tm, width), lambda b, i: (b, i, 0))
    per_seq = lambda rows, width: pl.BlockSpec((1, rows, width), lambda b, i: (b, 0, 0))
    consts = [a for a, _ in params]
    f32_tok = jax.ShapeDtypeStruct((B, S, w), F32)
    bf_tok = jax.ShapeDtypeStruct((B, S, w), BF16)
    realign_shape = (SUBLANES, tm + CONV_C_TAPS - 1, w) if sh % SUBLANES else (1, SUBLANES, w)
    if sample:
        kv_spec, kv_shape = tok(w), f32_tok
    else:
        kv_spec = pl.BlockSpec((1, w, tm), lambda b, i: (b, 0, i))
        kv_shape = jax.ShapeDtypeStruct((B, w, S), F32)
    nb_rows, nc_rows = (CONV_B_TAPS - 1) * sh, (CONV_C_TAPS - 1) * sh
    return pl.pallas_call(
        functools.partial(_mixer_body, sample=sample, tm=tm, sh=sh, hb=hb, hc=hc),
        grid=(B, nt),
        in_specs=[tok(D)] + [spec for _, spec in params] + [per_seq(hb, w), per_seq(hc, w)],
        out_specs=[tok(w)] * 4 + [kv_spec] * 2 + [tok(w)] * 3
                  + [per_seq(nb_rows, w), per_seq(nc_rows, w)],
        out_shape=[f32_tok] * 4 + [kv_shape] * 2 + [bf_tok] * 2 + [f32_tok]
                  + [jax.ShapeDtypeStruct((B, nb_rows, w), F32),
                     jax.ShapeDtypeStruct((B, nc_rows, w), F32)],
        scratch_shapes=[pltpu.VMEM((hb + tm, w), F32), pltpu.VMEM((hc + tm, w), F32),
                        pltpu.VMEM(realign_shape, F32)],
        compiler_params=pltpu.CompilerParams(
            dimension_semantics=("arbitrary", "arbitrary"), vmem_limit_bytes=VMEM_LIMIT),
        name="mixer_sample" if sample else "mixer_prompt",
    )(x, *consts, st_b, st_c)


def _suffix_matrix():
    n = SB_KEY_BLOCK
    r = lax.broadcasted_iota(jnp.int32, (n, n), 0)
    c = lax.broadcasted_iota(jnp.int32, (n, n), 1)
    return (r > c).astype(BF16)


_NT = (((1,), (1,)), ((), ()))


def _neg_abs(x):
    return pltpu.bitcast(pltpu.bitcast(x, jnp.uint32) | jnp.uint32(0x80000000), F32)


def _sb_tile(qm, kt, vt, bias_col, run, suffix, mask, *, keys_in_lanes, stack_blocks=False):
    if keys_in_lanes:
        z = _bdot(qm.astype(kt.dtype), kt) + bias_col
    else:
        z = lax.dot_general(qm.astype(kt.dtype), kt, _NT, preferred_element_type=F32) + bias_col
    sp = jnp.maximum(z, 0.0) + jnp.log(1.0 + jnp.exp(_neg_abs(z)))
    if mask is not None:
        sp = jnp.where(mask, sp, 0.0)
    m = z.shape[0]
    nblk = z.shape[1] // SB_KEY_BLOCK
    blocks = [sp[:, b * SB_KEY_BLOCK:(b + 1) * SB_KEY_BLOCK] for b in range(nblk)]
    if stack_blocks and nblk > 1:
        stacked = _bdot(jnp.concatenate(blocks, axis=0).astype(BF16), suffix)
        within = [stacked[b * m:(b + 1) * m, :] for b in range(nblk)]
    else:
        within = [_bdot(blk.astype(BF16), suffix) for blk in blocks]
    later = [None] * nblk
    for b in reversed(range(nblk)):
        later[b] = within[b] + run
        run = run + (within[b][:, 0:1] + blocks[b][:, 0:1])
    later = later[0] if nblk == 1 else jnp.concatenate(later, axis=1)
    a = jnp.exp(z - sp - later)
    if mask is not None:
        a = jnp.where(mask, a, 0.0)
    a = a.astype(BF16).astype(vt.dtype)
    if keys_in_lanes:
        return lax.dot_general(a, vt, _NT, preferred_element_type=F32), run
    return _bdot(a, vt), run


def _head_stack(q, qm_ref, rows):
    lane_head = lax.broadcasted_iota(jnp.int32, q.shape, 1) // HEAD_DIM
    for h in range(N_HEADS):
        qm_ref[h * rows:(h + 1) * rows, :] = jnp.where(lane_head == h, q, 0.0).astype(BF16)


def _head_unstack(acc_ref, rows):
    lane_head = lax.broadcasted_iota(jnp.int32, (rows, N_HEADS * HEAD_DIM), 1) // HEAD_DIM
    out = jnp.where(lane_head == 0, acc_ref[0:rows, :], 0.0)
    for h in range(1, N_HEADS):
        out = out + jnp.where(lane_head == h, acc_ref[h * rows:(h + 1) * rows, :], 0.0)
    return out


def _attn_prompt_body(q_ref, kb_ref, vb_ref, bias_ref, o_ref, qm_ref, acc_ref, run_ref, *, tq, wide):
    i = pl.program_id(1)
    m = N_HEADS * tq
    blk = SB_KEY_BLOCK
    _head_stack(q_ref[0] * (HEAD_DIM ** -0.5), qm_ref, tq)
    qm = qm_ref[...]
    bias_col = bias_ref[...]
    suffix = _suffix_matrix()

    def keys(start, n):
        start = pl.multiple_of(start, blk)
        return kb_ref[0, pl.ds(start, n), :], vb_ref[0, pl.ds(start, n), :]

    jd = (i * tq) // blk
    qpos = i * tq + (lax.broadcasted_iota(jnp.int32, (m, blk), 0) & (tq - 1))
    kpos = jd * blk + lax.broadcasted_iota(jnp.int32, (m, blk), 1)
    pv, run = _sb_tile(qm, *keys(jd * blk, blk), bias_col, jnp.zeros((m, 1), F32), suffix,
                       kpos < qpos, keys_in_lanes=False)
    acc_ref[...] = pv
    run_ref[...] = run

    def step(start, n):
        pv, run = _sb_tile(qm, *keys(start, n), bias_col, run_ref[...], suffix, None,
                           keys_in_lanes=False)
        acc_ref[...] += pv
        run_ref[...] = run

    per_wide = wide // blk
    n_single = jd % per_wide
    n_wide = jd // per_wide

    def single_body(n, carry):
        step((jd - 1 - n) * blk, blk)
        return carry

    def wide_body(n, carry):
        step((n_wide - 1 - n) * wide, wide)
        return carry

    lax.fori_loop(0, n_single, single_body, 0)
    lax.fori_loop(0, n_wide, wide_body, 0)
    o_ref[0] = _head_unstack(acc_ref, tq)


def _attn_prompt(q, kb, vb, sb_bias, *, tq=SB_KEY_BLOCK, wide=2 * SB_KEY_BLOCK):
    B, S, w = q.shape
    assert tq <= SB_KEY_BLOCK and SB_KEY_BLOCK % tq == 0
    m = N_HEADS * tq
    bias_rows = jnp.repeat(sb_bias.astype(F32), tq).reshape(m, 1)
    return pl.pallas_call(
        functools.partial(_attn_prompt_body, tq=tq, wide=wide),
        grid=(B, S // tq),
        in_specs=[pl.BlockSpec((1, tq, w), lambda b, i: (b, i, 0)),
                  pl.BlockSpec((1, S, w), lambda b, i: (b, 0, 0)),
                  pl.BlockSpec((1, S, w), lambda b, i: (b, 0, 0)),
                  _const_spec((m, 1))],
        out_specs=pl.BlockSpec((1, tq, w), lambda b, i: (b, i, 0)),
        out_shape=jax.ShapeDtypeStruct((B, S, w), F32),
        scratch_shapes=[pltpu.VMEM((m, w), BF16), pltpu.VMEM((m, w), F32), pltpu.VMEM((m, 1), F32)],
        compiler_params=pltpu.CompilerParams(
            dimension_semantics=("arbitrary", "arbitrary"), vmem_limit_bytes=VMEM_LIMIT),
        name="attend_prompt",
    )(q, kb, vb, bias_rows)


def _attn_sample_body(pt_ref, q_ref, kn_ref, vn_ref, bias_ref, *refs, tpad, page, npg, nsq):
    k_pages = refs[0:nsq * npg]
    v_pages = refs[nsq * npg:2 * nsq * npg]
    o_ref, qm_ref, acc_ref, run_ref = refs[2 * nsq * npg:]
    g = pl.program_id(1)
    m = N_HEADS * tpad
    bias_col = bias_ref[...]
    suffix = _suffix_matrix()

    @pl.when(g == 0)
    def _():
        nk = SB_KEY_BLOCK
        t_q = lax.broadcasted_iota(jnp.int32, (m, nk), 0) & (tpad - 1)
        t_k = lax.broadcasted_iota(jnp.int32, (m, nk), 1)
        for j in range(nsq):
            rows = slice(j * m, (j + 1) * m)
            _head_stack(q_ref[j] * (HEAD_DIM ** -0.5), qm_ref.at[rows], tpad)
            kn = jnp.concatenate([kn_ref[j], jnp.zeros((nk - tpad, kn_ref.shape[2]), F32)], axis=0)
            vn = jnp.concatenate([vn_ref[j], jnp.zeros((nk - tpad, vn_ref.shape[2]), F32)], axis=0)
            pv, run = _sb_tile(qm_ref[rows, :], kn.astype(BF16), vn.astype(BF16), bias_col,
                               jnp.zeros((m, 1), F32), suffix, t_k < t_q, keys_in_lanes=False)
            acc_ref[rows, :] = pv
            run_ref[rows, :] = run

    @pl.when(g > 0)
    def _():
        for j in range(nsq):
            rows = slice(j * m, (j + 1) * m)
            kt = jnp.concatenate([k_pages[j * npg + p][0, 0] for p in range(npg)], axis=1)
            vt = jnp.concatenate([v_pages[j * npg + p][0, 0] for p in range(npg)], axis=1)
            pv, run = _sb_tile(qm_ref[rows, :], kt, vt, bias_col, run_ref[rows, :], suffix, None,
                               keys_in_lanes=True, stack_blocks=True)
            acc_ref[rows, :] += pv
            run_ref[rows, :] = run

    @pl.when(g == pl.num_programs(1) - 1)
    def _():
        for j in range(nsq):
            o_ref[j] = _head_unstack(acc_ref.at[j * m:(j + 1) * m], tpad)


def _attn_sample(q, k_new, v_new, sb_bias, cache_kt, cache_vt, layer, page_table):
    n_seq, tpad, w = q.shape
    page = cache_kt.shape[3]
    n_pages = page_table.shape[1]
    npg = PAGES_PER_STEP
    nsq = SEQS_PER_STEP
    n_groups = n_pages // npg
    m = N_HEADS * tpad
    assert n_pages % npg == 0 and n_seq % nsq == 0
    bias_rows = jnp.repeat(sb_bias.astype(F32), tpad).reshape(m, 1)

    def page_spec(j, p):
        def index(s, g, pt):
            grp = n_groups - jnp.maximum(g, 1)
            return (layer, pt[(s * nsq + j) * n_pages + grp * npg + p], 0, 0)
        return pl.BlockSpec((1, 1, w, page), index)

    per_step = pl.BlockSpec((nsq, tpad, w), lambda s, g, pt: (s, 0, 0))
    page_specs = [page_spec(j, p) for j in range(nsq) for p in range(npg)]
    grid_spec = pltpu.PrefetchScalarGridSpec(
        num_scalar_prefetch=1,
        grid=(n_seq // nsq, n_groups + 1),
        in_specs=[per_step, per_step, per_step, pl.BlockSpec((m, 1), lambda s, g, pt: (0, 0))]
                 + page_specs * 2,
        out_specs=per_step,
        scratch_shapes=[pltpu.VMEM((nsq * m, w), BF16), pltpu.VMEM((nsq * m, w), F32),
                        pltpu.VMEM((nsq * m, 1), F32)],
    )
    n_in = nsq * npg
    return pl.pallas_call(
        functools.partial(_attn_sample_body, tpad=tpad, page=page, npg=npg, nsq=nsq),
        grid_spec=grid_spec,
        out_shape=jax.ShapeDtypeStruct((n_seq, tpad, w), F32),
        compiler_params=pltpu.CompilerParams(
            dimension_semantics=("arbitrary", "arbitrary"), vmem_limit_bytes=VMEM_LIMIT),
        name="attend_sample",
    )(page_table.reshape(-1), q, k_new, v_new, bias_rows, *([cache_kt] * n_in), *([cache_vt] * n_in))


def _merge_body(x_ref, oa_ref, ob_ref, oc_ref, od_ref, gmix_ref, wgate_ref, wbr_ref, wo_ref, xo_ref):
    x = x_ref[...]
    d = x.shape[1]
    h = _rms(x, gmix_ref[...]).astype(BF16)
    merged = None
    for kk, br_ref in enumerate((oa_ref, ob_ref, oc_ref, od_ref)):
        gate = jax.nn.sigmoid(_bdot(h, wgate_ref[0, :, kk * d:(kk + 1) * d]))
        term = gate * _bdot(br_ref[...].astype(BF16), wbr_ref[kk])
        merged = term if merged is None else merged + term
    xo_ref[...] = x + _bdot(merged.astype(BF16), wo_ref[...])


def _merge(x, oa, ob, oc, od, params, *, tm):
    n, d = x.shape
    w = BR_WIDTH
    tok = lambda width: pl.BlockSpec((tm, width), lambda i: (i, 0))
    consts = [a for a, _ in params]
    return pl.pallas_call(
        _merge_body,
        grid=(n // tm,),
        in_specs=[tok(d)] + [tok(w)] * 4 + [spec for _, spec in params],
        out_specs=tok(d),
        out_shape=jax.ShapeDtypeStruct((n, d), F32),
        compiler_params=pltpu.CompilerParams(
            dimension_semantics=("arbitrary",), vmem_limit_bytes=VMEM_LIMIT),
        name="merge",
    )(x, oa, ob, oc, od, *consts)


def _ffn_body(x_ref, p_ref, st_ref, gffn_ref, wup_ref, cw_ref, wdn_ref, gple_ref, wpg_ref,
              wple_ref, gfin_ref, xo_ref, nf_ref, ubuf, act_ref, *, tm, sh, hh, final):
    i = pl.program_id(1)

    @pl.when(i == 0)
    def _():
        ubuf[0:hh, :] = st_ref[0]

    x = x_ref[0]
    hn = _rms(x, gffn_ref[...]).astype(BF16)
    d_ff = wdn_ref.shape[0]
    cw = FFN_CHUNK
    ubuf[hh:hh + tm, :] = _bdot(hn, wup_ref[...])
    base = hh - (FFN_TAPS - 1) * sh
    for c in range(d_ff // cw):
        conv = []
        for col in (c * cw, d_ff + c * cw):
            cv = cw_ref[0:1, col:col + cw] * ubuf[base:base + tm, col:col + cw]
            for j in range(1, FFN_TAPS):
                cv = cv + cw_ref[j:j + 1, col:col + cw] * ubuf[base + j * sh:base + j * sh + tm, col:col + cw]
            conv.append(cv)
        act_ref[:, c * cw:(c + 1) * cw] = (jax.nn.silu(conv[0]) * conv[1]).astype(BF16)
    nf_ref[0] = ubuf[hh + tm - (FFN_TAPS - 1) * sh:hh + tm, :]
    ubuf[0:hh, :] = ubuf[tm:tm + hh, :]
    x = x + _bdot(act_ref[...], wdn_ref[...])
    gate = jax.nn.sigmoid(_bdot(_rms(x, gple_ref[...]).astype(BF16), wpg_ref[...]))
    x = x + gate * _bdot(p_ref[0].astype(BF16), wple_ref[...])
    xo_ref[0] = _rms(x, gfin_ref[...]) if final else x


def _ffn(x, p, st_f, params, *, tm, sh, final):
    B, S, D = x.shape
    hh, up_cols = st_f.shape[1], st_f.shape[2]
    tok = lambda width: pl.BlockSpec((1, tm, width), lambda b, i: (b, i, 0))
    consts = [a for a, _ in params]
    p, p_spec = p
    nf_rows = (FFN_TAPS - 1) * sh
    return pl.pallas_call(
        functools.partial(_ffn_body, tm=tm, sh=sh, hh=hh, final=final),
        grid=(B, S // tm),
        in_specs=[tok(D), p_spec, pl.BlockSpec((1, hh, up_cols), lambda b, i: (b, 0, 0))]
                 + [spec for _, spec in params],
        out_specs=[tok(D), pl.BlockSpec((1, nf_rows, up_cols), lambda b, i: (b, 0, 0))],
        out_shape=[jax.ShapeDtypeStruct((B, S, D), F32),
                   jax.ShapeDtypeStruct((B, nf_rows, up_cols), F32)],
        scratch_shapes=[pltpu.VMEM((hh + tm, up_cols), F32), pltpu.VMEM((tm, up_cols // 2), BF16)],
        compiler_params=pltpu.CompilerParams(
            dimension_semantics=("arbitrary", "arbitrary"), vmem_limit_bytes=VMEM_LIMIT),
        name="ffn",
    )(x, p, st_f, *consts)


def _token_major(a):
    n_seq, t, c = a.shape
    return a.transpose(1, 0, 2).reshape(1, t * n_seq, c)


def _seq_major(a, n_seq):
    c = a.shape[-1]
    return a.reshape(-1, n_seq, c).transpose(1, 0, 2)


def kernel(x_prompt, x_sample, p_prompt, p_sample, cache_k, cache_v, page_table, state_conv_b, state_conv_c, state_ffn_conv, g_mix, w_in, ln_v_g, ln_v_b, w_sp, b_sp, conv_b_w, conv_c_w, ln_c_g, ln_c_b, sb_bias, w_br, w_o, g_ffn, w_up, ffn_conv_w, w_down, g_ple, w_ple_gate, w_ple, g_final):
    depth = w_in.shape[0]
    B, S, D = x_prompt.shape
    n_seq, T, _ = x_sample.shape
    w = BR_WIDTH
    n_grp = w_sp.shape[1]
    chunk = w_sp.shape[2]
    tm = min(512, S)
    tpad = 8
    assert S % tm == 0 and tm % chunk == 0 and S % SB_KEY_BLOCK == 0 and T <= tpad

    xp = x_prompt
    xs = _token_major(x_sample)
    zeros_b = jnp.zeros((B, 8, w), F32)
    zeros_c = jnp.zeros((B, 32, w), F32)
    zeros_f = jnp.zeros((B, 8, w_up.shape[2]), F32)
    n_phys, page = cache_k.shape[1], cache_k.shape[2]
    cache_kt = cache_k.transpose(0, 1, 3, 4, 2).reshape(depth, n_phys, w, page)
    cache_vt = cache_v.transpose(0, 1, 3, 4, 2).reshape(depth, n_phys, w, page)

    w_in_b = w_in.astype(BF16)
    w_br_b = w_br.astype(BF16)
    w_o_b = w_o.astype(BF16)
    w_up_b = w_up.astype(BF16)
    w_down_b = w_down.astype(BF16)
    w_pg_b = w_ple_gate.astype(BF16)
    w_ple_b = w_ple.astype(BF16)
    grp_cols = w // n_grp
    ple_dim = p_prompt.shape[-1]

    outs = {name: [] for name in ("kp", "vp", "cbp", "ccp", "cfp", "ks", "vs", "cvs", "cbs", "ccs", "cfs")}
    for l in range(depth):
        final = l == depth - 1
        lp = functools.partial(_layer_param, layer=l)
        mixer_shared = [lp(g_mix), lp(w_in_b, cols=(0, MIX_COLS)), lp(ln_v_g), lp(ln_v_b)]
        mixer_conv = [lp(conv_b_w), lp(conv_c_w), lp(ln_c_g), lp(ln_c_b)]
        merge_params = [lp(g_mix), lp(w_in_b, cols=(MIX_COLS, w_in.shape[2] - MIX_COLS)),
                        lp(w_br_b), lp(w_o_b)]
        ffn_params = [lp(g_ffn), lp(w_up_b), lp(ffn_conv_w), lp(w_down_b), lp(g_ple), lp(w_pg_b),
                      lp(w_ple_b), _whole(g_final.reshape(1, -1))]

        bmat = jnp.repeat(b_sp[l][:, :chunk].T, grp_cols, axis=1)
        (oa, ob, oc, q, k, v, kb, vb, _, nb, nc) = _mixer(
            xp, mixer_shared + [lp(w_sp), _whole(bmat)] + mixer_conv, zeros_b, zeros_c,
            sample=False, tm=tm, sh=1)
        od = _attn_prompt(q, kb, vb, sb_bias[l])
        flat = lambda a: a.reshape(B * S, a.shape[-1])
        x1 = _merge(flat(xp), flat(oa), flat(ob), flat(oc), flat(od), merge_params,
                    tm=tm).reshape(B, S, D)
        p_spec = pl.BlockSpec((None, 1, tm, ple_dim), lambda b, i, l=l: (l, b, i, 0))
        xp, nf = _ffn(x1, (p_prompt, p_spec), zeros_f, ffn_params, tm=tm, sh=1, final=final)
        heads_last = lambda t: t.reshape(B, N_HEADS, HEAD_DIM, S).transpose(0, 3, 1, 2)
        outs["kp"].append(heads_last(k))
        outs["vp"].append(heads_last(v))
        outs["cbp"].append(nb)
        outs["ccp"].append(nc)
        outs["cfp"].append(nf)

        wv = jnp.repeat(w_sp[l][:, :T, :T].transpose(1, 2, 0).reshape(T * T, n_grp), grp_cols, axis=1)
        bv = jnp.repeat(b_sp[l][:, :T].T, grp_cols, axis=1)
        (oa, ob, oc, q, k, v, _, _, va, nb, nc) = _mixer(
            xs, mixer_shared + [_whole(wv), _whole(bv)] + mixer_conv,
            _token_major(state_conv_b[l]), _token_major(state_conv_c[l]),
            sample=True, tm=T * n_seq, sh=n_seq)
        pad_t = lambda a: jnp.pad(_seq_major(a, n_seq), ((0, 0), (0, tpad - T), (0, 0)))
        od = _attn_sample(pad_t(q), pad_t(k), pad_t(v), sb_bias[l], cache_kt, cache_vt, l, page_table)
        od = _token_major(od[:, :T, :])
        x1 = _merge(xs[0], oa[0], ob[0], oc[0], od[0], merge_params, tm=T * n_seq)[None]
        ps = _token_major(p_sample[l])
        p_spec = pl.BlockSpec((1, T * n_seq, ple_dim), lambda b, i: (b, i, 0))
        xs, nf = _ffn(x1, (ps, p_spec), _token_major(state_ffn_conv[l]), ffn_params,
                      tm=T * n_seq, sh=n_seq, final=final)
        outs["ks"].append(_seq_major(k, n_seq).reshape(n_seq, T, N_HEADS, HEAD_DIM))
        outs["vs"].append(_seq_major(v, n_seq).reshape(n_seq, T, N_HEADS, HEAD_DIM))
        outs["cvs"].append(_seq_major(va, n_seq))
        outs["cbs"].append(_seq_major(nb, n_seq))
        outs["ccs"].append(_seq_major(nc, n_seq))
        outs["cfs"].append(_seq_major(nf, n_seq))

    st = lambda name: jnp.stack(outs[name])
    return (xp, _seq_major(xs, n_seq), st("kp"), st("vp"), st("cbp"), st("ccp"), st("cfp"),
            st("ks"), st("vs"), st("cvs"), st("cbs"), st("ccs"), st("cfs"))
```

```python
import functools

import jax
import jax.numpy as jnp
from jax import lax
from jax.experimental import pallas as pl
from jax.experimental.pallas import tpu as pltpu

F32 = jnp.float32
BF16 = jnp.bfloat16

EPS = 1e-6
SUBLANES = 8
BR_WIDTH = 256
N_HEADS = 4
HEAD_DIM = 64
MIX_COLS = 10 * BR_WIDTH
CONV_B_TAPS = 3
CONV_C_TAPS = 31
FFN_TAPS = 3
FFN_CHUNK = 256
SB_KEY_BLOCK = 256
PAGES_PER_STEP = 16
SEQS_PER_STEP = 2
VMEM_LIMIT = 56 * 1024 * 1024


def _rms(x, g):
    return x * lax.rsqrt(jnp.mean(x * x, axis=-1, keepdims=True) + EPS) * g


def _layernorm(x, g, b):
    mu = jnp.mean(x, axis=-1, keepdims=True)
    xc = x - mu
    var = jnp.mean(xc * xc, axis=-1, keepdims=True)
    return xc * lax.rsqrt(var + EPS) * g + b


def _bdot(a, b):
    return jnp.dot(a, b, preferred_element_type=F32)


def _const_spec(shape):
    nd = len(shape)
    return pl.BlockSpec(shape, lambda *_: (0,) * nd, pipeline_mode=pl.Buffered(1))


def _whole(a):
    return a, _const_spec(a.shape)


def _layer_param(a, layer, cols=None):
    if a.ndim == 2:
        a = a[:, None, :]
    if cols is None:
        block = (None,) + a.shape[1:]
        index = (layer,) + (0,) * (a.ndim - 1)
    else:
        block = (pl.Element(1),) + tuple(pl.Element(n) for n in a.shape[1:-1]) + (pl.Element(cols[1]),)
        index = (layer,) + (0,) * (a.ndim - 2) + (cols[0],)
    return a, pl.BlockSpec(block, lambda *_: index, pipeline_mode=pl.Buffered(1))


def _mixer_body(x_ref, gmix_ref, wmix_ref, lnvg_ref, lnvb_ref, wsp_ref, bsp_ref,
                cbw_ref, ccw_ref, lncg_ref, lncb_ref, stb_ref, stc_ref,
                oa_ref, ob_ref, oc_ref, q_ref, k_ref, v_ref, kb_ref, vb_ref,
                va_ref, nb_ref, nc_ref, bbuf, cbuf, wbuf, *, sample, tm, sh, hb, hc):
    i = pl.program_id(1)

    @pl.when(i == 0)
    def _():
        bbuf[0:hb, :] = stb_ref[0]
        cbuf[0:hc, :] = stc_ref[0]

    w = BR_WIDTH
    h = _rms(x_ref[0], gmix_ref[...]).astype(BF16)
    y = _bdot(h, wmix_ref[0])

    ga = jax.nn.gelu(y[:, 0:2 * w])
    u = ga[:, 0:w]
    va = _layernorm(ga[:, w:2 * w], lnvg_ref[...], lnvb_ref[...])
    va_ref[0] = va
    parts = []
    if sample:
        nt = tm // sh
        for t in range(nt):
            s = jnp.broadcast_to(bsp_ref[t:t + 1, :], (sh, w))
            for s2 in range(t + 1):
                s = s + wsp_ref[t * nt + s2:t * nt + s2 + 1, :] * va[s2 * sh:(s2 + 1) * sh, :]
            parts.append(s)
    else:
        ch = wsp_ref.shape[1]
        rr = lax.broadcasted_iota(jnp.int32, (ch, ch), 0)
        cc = lax.broadcasted_iota(jnp.int32, (ch, ch), 1)
        grp = lax.broadcasted_iota(jnp.int32, (ch, w), 1) // (w // wsp_ref.shape[0])
        wg = [jnp.where(cc <= rr, wsp_ref[g], 0.0).astype(BF16) for g in range(wsp_ref.shape[0])]
        for c in range(tm // ch):
            vc = va[c * ch:(c + 1) * ch, :]
            s = bsp_ref[...]
            for g in range(wsp_ref.shape[0]):
                s = s + _bdot(wg[g], jnp.where(grp == g, vc, 0.0).astype(BF16))
            parts.append(s)
    oa_ref[0] = u * jnp.concatenate(parts, axis=0)

    bbuf[hb:hb + tm, :] = y[:, 3 * w:4 * w] * y[:, 4 * w:5 * w]
    base = hb - (CONV_B_TAPS - 1) * sh
    cb = cbw_ref[0:1, :] * bbuf[base:base + tm, :]
    for j in range(1, CONV_B_TAPS):
        cb = cb + cbw_ref[j:j + 1, :] * bbuf[base + j * sh:base + j * sh + tm, :]
    ob_ref[0] = y[:, 2 * w:3 * w] * cb
    nb_ref[0] = bbuf[hb + tm - (CONV_B_TAPS - 1) * sh:hb + tm, :]
    bbuf[0:hb, :] = bbuf[tm:tm + hb, :]

    cbuf[hc:hc + tm, :] = y[:, 5 * w:6 * w] * jax.nn.sigmoid(y[:, 6 * w:7 * w])
    base = hc - (CONV_C_TAPS - 1) * sh
    cv = None
    residues = SUBLANES if sh % SUBLANES else 1
    for b in range(residues):
        taps = list(range(b, CONV_C_TAPS, residues))
        span = (taps[-1] - b) * sh
        start = base + b * sh
        if start % SUBLANES:
            win = wbuf.at[b]
            win[0:span + tm, :] = cbuf[start:start + span + tm, :]
            start = 0
        else:
            win = cbuf
        for j in taps:
            off = start + (j - b) * sh
            term = ccw_ref[j:j + 1, :] * win[off:off + tm, :]
            cv = term if cv is None else cv + term
    oc_ref[0] = jax.nn.silu(_layernorm(cv, lncg_ref[...], lncb_ref[...]))
    nc_ref[0] = cbuf[hc + tm - (CONV_C_TAPS - 1) * sh:hc + tm, :]
    cbuf[0:hc, :] = cbuf[tm:tm + hc, :]

    q = y[:, 7 * w:8 * w]
    k = y[:, 8 * w:9 * w]
    v = y[:, 9 * w:10 * w]
    q_ref[0] = q
    k_ref[0] = k if sample else k.T
    v_ref[0] = v if sample else v.T
    kb_ref[0] = k.astype(BF16)
    vb_ref[0] = v.astype(BF16)


def _without_refs(body, at, n):
    def wrapped(*refs, **kwargs):
        return body(*refs[:at], *refs[at + n:], **kwargs)
    return wrapped


def _mixer(x, params, st_b, st_c, *, sample, tm, sh, kv_stack=None):
    B, S, D = x.shape
    w = BR_WIDTH
    hb, hc = st_b.shape[1], st_c.shape[1]
    nt = S // tm
    tok = lambda width: pl.BlockSpec((1, tm, width), lambda b, i: (b, i, 0))
    per_seq = lambda rows, width: pl.BlockSpec((1, rows, width), lambda b, i: (b, 0, 0))
    consts = [a for a, _ in params]
    f32_tok = jax.ShapeDtypeStruct((B, S, w), F32)
    bf_tok = jax.ShapeDtypeStruct((B, S, w), BF16)
    realign_shape = (SUBLANES, tm + CONV_C_TAPS - 1, w) if sh % SUBLANES else (1, SUBLANES, w)
    body = functools.partial(_mixer_body, sample=sample, tm=tm, sh=sh, hb=hb, hc=hc)
    stacks, stack_specs, aliases = [], [], {}
    if sample:
        kv_spec, kv_shape = tok(w), f32_tok
    else:
        layer, depth, previous = kv_stack
        kv_spec = pl.BlockSpec((None, 1, w, tm), lambda b, i: (layer, b, 0, i))
        kv_shape = jax.ShapeDtypeStruct((depth, B, w, S), F32)
        if previous is not None:
            n_in = 1 + len(consts) + 2
            stacks, stack_specs = list(previous), [pl.BlockSpec(memory_space=pl.ANY)] * 2
            aliases = {n_in: 4, n_in + 1: 5}
            body = _without_refs(body, n_in, 2)
    nb_rows, nc_rows = (CONV_B_TAPS - 1) * sh, (CONV_C_TAPS - 1) * sh
    return pl.pallas_call(
        body,
        grid=(B, nt),
        in_specs=[tok(D)] + [spec for _, spec in params] + [per_seq(hb, w), per_seq(hc, w)]
                 + stack_specs,
        input_output_aliases=aliases,
        out_specs=[tok(w)] * 4 + [kv_spec] * 2 + [tok(w)] * 3
                  + [per_seq(nb_rows, w), per_seq(nc_rows, w)],
        out_shape=[f32_tok] * 4 + [kv_shape] * 2 + [bf_tok] * 2 + [f32_tok]
                  + [jax.ShapeDtypeStruct((B, nb_rows, w), F32),
                     jax.ShapeDtypeStruct((B, nc_rows, w), F32)],
        scratch_shapes=[pltpu.VMEM((hb + tm, w), F32), pltpu.VMEM((hc + tm, w), F32),
                        pltpu.VMEM(realign_shape, F32)],
        compiler_params=pltpu.CompilerParams(
            dimension_semantics=("arbitrary", "arbitrary"), vmem_limit_bytes=VMEM_LIMIT),
        name="mixer_sample" if sample else "mixer_prompt",
    )(x, *consts, st_b, st_c, *stacks)


def _suffix_matrix():
    n = SB_KEY_BLOCK
    r = lax.broadcasted_iota(jnp.int32, (n, n), 0)
    c = lax.broadcasted_iota(jnp.int32, (n, n), 1)
    return (r > c).astype(BF16)


_NT = (((1,), (1,)), ((), ()))


def _neg_abs(x):
    return pltpu.bitcast(pltpu.bitcast(x, jnp.uint32) | jnp.uint32(0x80000000), F32)


def _sb_tile(qm, kt, vt, bias_col, run, suffix, mask, *, keys_in_lanes, stack_blocks=False):
    if keys_in_lanes:
        z = _bdot(qm.astype(kt.dtype), kt) + bias_col
    else:
        z = lax.dot_general(qm.astype(kt.dtype), kt, _NT, preferred_element_type=F32) + bias_col
    sp = jnp.maximum(z, 0.0) + jnp.log(1.0 + jnp.exp(_neg_abs(z)))
    if mask is not None:
        sp = jnp.where(mask, sp, 0.0)
    m = z.shape[0]
    nblk = z.shape[1] // SB_KEY_BLOCK
    blocks = [sp[:, b * SB_KEY_BLOCK:(b + 1) * SB_KEY_BLOCK] for b in range(nblk)]
    if stack_blocks and nblk > 1:
        stacked = _bdot(jnp.concatenate(blocks, axis=0).astype(BF16), suffix)
        within = [stacked[b * m:(b + 1) * m, :] for b in range(nblk)]
    else:
        within = [_bdot(blk.astype(BF16), suffix) for blk in blocks]
    later = [None] * nblk
    for b in reversed(range(nblk)):
        later[b] = within[b] + run
        run = run + (within[b][:, 0:1] + blocks[b][:, 0:1])
    later = later[0] if nblk == 1 else jnp.concatenate(later, axis=1)
    a = jnp.exp(z - sp - later)
    if mask is not None:
        a = jnp.where(mask, a, 0.0)
    a = a.astype(BF16).astype(vt.dtype)
    if keys_in_lanes:
        return lax.dot_general(a, vt, _NT, preferred_element_type=F32), run
    return _bdot(a, vt), run


def _head_stack(q, qm_ref, rows):
    lane_head = lax.broadcasted_iota(jnp.int32, q.shape, 1) // HEAD_DIM
    for h in range(N_HEADS):
        qm_ref[h * rows:(h + 1) * rows, :] = jnp.where(lane_head == h, q, 0.0).astype(BF16)


def _head_unstack(acc_ref, rows):
    lane_head = lax.broadcasted_iota(jnp.int32, (rows, N_HEADS * HEAD_DIM), 1) // HEAD_DIM
    out = jnp.where(lane_head == 0, acc_ref[0:rows, :], 0.0)
    for h in range(1, N_HEADS):
        out = out + jnp.where(lane_head == h, acc_ref[h * rows:(h + 1) * rows, :], 0.0)
    return out


def _attn_prompt_body(q_ref, kb_ref, vb_ref, bias_ref, o_ref, qm_ref, acc_ref, run_ref, *, tq, wide):
    i = pl.program_id(1)
    m = N_HEADS * tq
    blk = SB_KEY_BLOCK
    _head_stack(q_ref[0] * (HEAD_DIM ** -0.5), qm_ref, tq)
    qm = qm_ref[...]
    bias_col = bias_ref[...]
    suffix = _suffix_matrix()

    def keys(start, n):
        start = pl.multiple_of(start, blk)
        return kb_ref[0, pl.ds(start, n), :], vb_ref[0, pl.ds(start, n), :]

    t_q = lax.broadcasted_iota(jnp.int32, (m, tq), 0) & (tq - 1)
    t_k = lax.broadcasted_iota(jnp.int32, (m, tq), 1)
    pv, run = _sb_tile(qm, *keys(i * tq, tq), bias_col, jnp.zeros((m, 1), F32), suffix,
                       t_k < t_q, keys_in_lanes=False)
    acc_ref[...] = pv
    run_ref[...] = run

    def step(start, n):
        pv, run = _sb_tile(qm, *keys(start, n), bias_col, run_ref[...], suffix, None,
                           keys_in_lanes=False)
        acc_ref[...] += pv
        run_ref[...] = run

    per_wide = wide // blk
    jd = (i * tq) // blk
    n_single = jd % per_wide
    n_wide = jd // per_wide

    def single_body(n, carry):
        step((jd - 1 - n) * blk, blk)
        return carry

    def wide_body(n, carry):
        step((n_wide - 1 - n) * wide, wide)
        return carry

    lax.fori_loop(0, n_single, single_body, 0)
    lax.fori_loop(0, n_wide, wide_body, 0)
    o_ref[0] = _head_unstack(acc_ref, tq)


def _attn_prompt(q, kb, vb, sb_bias, *, tq=2 * SB_KEY_BLOCK, wide=2 * SB_KEY_BLOCK):
    B, S, w = q.shape
    tq = min(tq, S)
    assert tq % SB_KEY_BLOCK == 0 and tq & (tq - 1) == 0 and S % tq == 0
    m = N_HEADS * tq
    bias_rows = jnp.repeat(sb_bias.astype(F32), tq).reshape(m, 1)
    return pl.pallas_call(
        functools.partial(_attn_prompt_body, tq=tq, wide=wide),
        grid=(B, S // tq),
        in_specs=[pl.BlockSpec((1, tq, w), lambda b, i: (b, i, 0)),
                  pl.BlockSpec((1, S, w), lambda b, i: (b, 0, 0)),
                  pl.BlockSpec((1, S, w), lambda b, i: (b, 0, 0)),
                  _const_spec((m, 1))],
        out_specs=pl.BlockSpec((1, tq, w), lambda b, i: (b, i, 0)),
        out_shape=jax.ShapeDtypeStruct((B, S, w), F32),
        scratch_shapes=[pltpu.VMEM((m, w), BF16), pltpu.VMEM((m, w), F32), pltpu.VMEM((m, 1), F32)],
        compiler_params=pltpu.CompilerParams(
            dimension_semantics=("arbitrary", "arbitrary"), vmem_limit_bytes=VMEM_LIMIT),
        name="attend_prompt",
    )(q, kb, vb, bias_rows)


def _attn_sample_body(pt_ref, q_ref, kn_ref, vn_ref, bias_ref, *refs, tpad, page, npg, nsq):
    k_pages = refs[0:nsq * npg]
    v_pages = refs[nsq * npg:2 * nsq * npg]
    o_ref, qm_ref, acc_ref, run_ref = refs[2 * nsq * npg:]
    g = pl.program_id(1)
    m = N_HEADS * tpad
    bias_col = bias_ref[...]
    suffix = _suffix_matrix()

    @pl.when(g == 0)
    def _():
        nk = SB_KEY_BLOCK
        t_q = lax.broadcasted_iota(jnp.int32, (m, nk), 0) & (tpad - 1)
        t_k = lax.broadcasted_iota(jnp.int32, (m, nk), 1)
        for j in range(nsq):
            rows = slice(j * m, (j + 1) * m)
            _head_stack(q_ref[j] * (HEAD_DIM ** -0.5), qm_ref.at[rows], tpad)
            kn = jnp.concatenate([kn_ref[j], jnp.zeros((nk - tpad, kn_ref.shape[2]), F32)], axis=0)
            vn = jnp.concatenate([vn_ref[j], jnp.zeros((nk - tpad, vn_ref.shape[2]), F32)], axis=0)
            pv, run = _sb_tile(qm_ref[rows, :], kn.astype(BF16), vn.astype(BF16), bias_col,
                               jnp.zeros((m, 1), F32), suffix, t_k < t_q, keys_in_lanes=False)
            acc_ref[rows, :] = pv
            run_ref[rows, :] = run

    @pl.when(g > 0)
    def _():
        for j in range(nsq):
            rows = slice(j * m, (j + 1) * m)
            kt = jnp.concatenate([k_pages[j * npg + p][0, 0] for p in range(npg)], axis=1)
            vt = jnp.concatenate([v_pages[j * npg + p][0, 0] for p in range(npg)], axis=1)
            pv, run = _sb_tile(qm_ref[rows, :], kt, vt, bias_col, run_ref[rows, :], suffix, None,
                               keys_in_lanes=True, stack_blocks=True)
            acc_ref[rows, :] += pv
            run_ref[rows, :] = run

    @pl.when(g == pl.num_programs(1) - 1)
    def _():
        for j in range(nsq):
            o_ref[j] = _head_unstack(acc_ref.at[j * m:(j + 1) * m], tpad)


def _attn_sample(q, k_new, v_new, sb_bias, cache_kt, cache_vt, layer, page_table):
    n_seq, tpad, w = q.shape
    page = cache_kt.shape[3]
    n_pages = page_table.shape[1]
    npg = PAGES_PER_STEP
    nsq = SEQS_PER_STEP
    n_groups = n_pages // npg
    m = N_HEADS * tpad
    assert n_pages % npg == 0 and n_seq % nsq == 0
    bias_rows = jnp.repeat(sb_bias.astype(F32), tpad).reshape(m, 1)

    def page_spec(j, p):
        def index(s, g, pt):
            grp = n_groups - jnp.maximum(g, 1)
            return (layer, pt[(s * nsq + j) * n_pages + grp * npg + p], 0, 0)
        return pl.BlockSpec((1, 1, w, page), index)

    per_step = pl.BlockSpec((nsq, tpad, w), lambda s, g, pt: (s, 0, 0))
    page_specs = [page_spec(j, p) for j in range(nsq) for p in range(npg)]
    grid_spec = pltpu.PrefetchScalarGridSpec(
        num_scalar_prefetch=1,
        grid=(n_seq // nsq, n_groups + 1),
        in_specs=[per_step, per_step, per_step, pl.BlockSpec((m, 1), lambda s, g, pt: (0, 0))]
                 + page_specs * 2,
        out_specs=per_step,
        scratch_shapes=[pltpu.VMEM((nsq * m, w), BF16), pltpu.VMEM((nsq * m, w), F32),
                        pltpu.VMEM((nsq * m, 1), F32)],
    )
    n_in = nsq * npg
    return pl.pallas_call(
        functools.partial(_attn_sample_body, tpad=tpad, page=page, npg=npg, nsq=nsq),
        grid_spec=grid_spec,
        out_shape=jax.ShapeDtypeStruct((n_seq, tpad, w), F32),
        compiler_params=pltpu.CompilerParams(
            dimension_semantics=("arbitrary", "arbitrary"), vmem_limit_bytes=VMEM_LIMIT),
        name="attend_sample",
    )(page_table.reshape(-1), q, k_new, v_new, bias_rows, *([cache_kt] * n_in), *([cache_vt] * n_in))


def _merge_body(x_ref, oa_ref, ob_ref, oc_ref, od_ref, gmix_ref, wgate_ref, wbr_ref, wo_ref, xo_ref):
    x = x_ref[...]
    d = x.shape[1]
    h = _rms(x, gmix_ref[...]).astype(BF16)
    merged = None
    for kk, br_ref in enumerate((oa_ref, ob_ref, oc_ref, od_ref)):
        gate = jax.nn.sigmoid(_bdot(h, wgate_ref[0, :, kk * d:(kk + 1) * d]))
        term = gate * _bdot(br_ref[...].astype(BF16), wbr_ref[kk])
        merged = term if merged is None else merged + term
    xo_ref[...] = x + _bdot(merged.astype(BF16), wo_ref[...])


def _merge(x, oa, ob, oc, od, params, *, tm):
    n, d = x.shape
    w = BR_WIDTH
    tok = lambda width: pl.BlockSpec((tm, width), lambda i: (i, 0))
    consts = [a for a, _ in params]
    return pl.pallas_call(
        _merge_body,
        grid=(n // tm,),
        in_specs=[tok(d)] + [tok(w)] * 4 + [spec for _, spec in params],
        out_specs=tok(d),
        out_shape=jax.ShapeDtypeStruct((n, d), F32),
        compiler_params=pltpu.CompilerParams(
            dimension_semantics=("arbitrary",), vmem_limit_bytes=VMEM_LIMIT),
        name="merge",
    )(x, oa, ob, oc, od, *consts)


def _ffn_body(x_ref, p_ref, st_ref, gffn_ref, wup_ref, cw_ref, wdn_ref, gple_ref, wpg_ref,
              wple_ref, gfin_ref, xo_ref, nf_ref, ubuf, act_ref, *, tm, sh, hh, final):
    i = pl.program_id(1)

    @pl.when(i == 0)
    def _():
        ubuf[0:hh, :] = st_ref[0]

    x = x_ref[0]
    hn = _rms(x, gffn_ref[...]).astype(BF16)
    d_ff = wdn_ref.shape[0]
    cw = FFN_CHUNK
    ubuf[hh:hh + tm, :] = _bdot(hn, wup_ref[...])
    base = hh - (FFN_TAPS - 1) * sh
    for c in range(d_ff // cw):
        conv = []
        for col in (c * cw, d_ff + c * cw):
            cv = cw_ref[0:1, col:col + cw] * ubuf[base:base + tm, col:col + cw]
            for j in range(1, FFN_TAPS):
                cv = cv + cw_ref[j:j + 1, col:col + cw] * ubuf[base + j * sh:base + j * sh + tm, col:col + cw]
            conv.append(cv)
        act_ref[:, c * cw:(c + 1) * cw] = (jax.nn.silu(conv[0]) * conv[1]).astype(BF16)
    nf_ref[0] = ubuf[hh + tm - (FFN_TAPS - 1) * sh:hh + tm, :]
    ubuf[0:hh, :] = ubuf[tm:tm + hh, :]
    x = x + _bdot(act_ref[...], wdn_ref[...])
    gate = jax.nn.sigmoid(_bdot(_rms(x, gple_ref[...]).astype(BF16), wpg_ref[...]))
    x = x + gate * _bdot(p_ref[0].astype(BF16), wple_ref[...])
    xo_ref[0] = _rms(x, gfin_ref[...]) if final else x


def _ffn(x, p, st_f, params, *, tm, sh, final):
    B, S, D = x.shape
    hh, up_cols = st_f.shape[1], st_f.shape[2]
    tok = lambda width: pl.BlockSpec((1, tm, width), lambda b, i: (b, i, 0))
    consts = [a for a, _ in params]
    p, p_spec = p
    nf_rows = (FFN_TAPS - 1) * sh
    return pl.pallas_call(
        functools.partial(_ffn_body, tm=tm, sh=sh, hh=hh, final=final),
        grid=(B, S // tm),
        in_specs=[tok(D), p_spec, pl.BlockSpec((1, hh, up_cols), lambda b, i: (b, 0, 0))]
                 + [spec for _, spec in params],
        out_specs=[tok(D), pl.BlockSpec((1, nf_rows, up_cols), lambda b, i: (b, 0, 0))],
        out_shape=[jax.ShapeDtypeStruct((B, S, D), F32),
                   jax.ShapeDtypeStruct((B, nf_rows, up_cols), F32)],
        scratch_shapes=[pltpu.VMEM((hh + tm, up_cols), F32), pltpu.VMEM((tm, up_cols // 2), BF16)],
        compiler_params=pltpu.CompilerParams(
            dimension_semantics=("arbitrary", "arbitrary"), vmem_limit_bytes=VMEM_LIMIT),
        name="ffn",
    )(x, p, st_f, *consts)


def _token_major(a):
    n_seq, t, c = a.shape
    return a.transpose(1, 0, 2).reshape(1, t * n_seq, c)


def _seq_major(a, n_seq):
    c = a.shape[-1]
    return a.reshape(-1, n_seq, c).transpose(1, 0, 2)


def kernel(x_prompt, x_sample, p_prompt, p_sample, cache_k, cache_v, page_table, state_conv_b, state_conv_c, state_ffn_conv, g_mix, w_in, ln_v_g, ln_v_b, w_sp, b_sp, conv_b_w, conv_c_w, ln_c_g, ln_c_b, sb_bias, w_br, w_o, g_ffn, w_up, ffn_conv_w, w_down, g_ple, w_ple_gate, w_ple, g_final):
    depth = w_in.shape[0]
    B, S, D = x_prompt.shape
    n_seq, T, _ = x_sample.shape
    w = BR_WIDTH
    n_grp = w_sp.shape[1]
    chunk = w_sp.shape[2]
    tm = min(512, S)
    tpad = 8
    assert S % tm == 0 and tm % chunk == 0 and S % SB_KEY_BLOCK == 0 and T <= tpad

    xp = x_prompt
    xs = _token_major(x_sample)
    zeros_b = jnp.zeros((B, 8, w), F32)
    zeros_c = jnp.zeros((B, 32, w), F32)
    zeros_f = jnp.zeros((B, 8, w_up.shape[2]), F32)
    n_phys, page = cache_k.shape[1], cache_k.shape[2]
    cache_kt = cache_k.transpose(0, 1, 3, 4, 2).reshape(depth, n_phys, w, page)
    cache_vt = cache_v.transpose(0, 1, 3, 4, 2).reshape(depth, n_phys, w, page)

    w_in_b = w_in.astype(BF16)
    w_br_b = w_br.astype(BF16)
    w_o_b = w_o.astype(BF16)
    w_up_b = w_up.astype(BF16)
    w_down_b = w_down.astype(BF16)
    w_pg_b = w_ple_gate.astype(BF16)
    w_ple_b = w_ple.astype(BF16)
    grp_cols = w // n_grp
    ple_dim = p_prompt.shape[-1]

    outs = {name: [] for name in ("cbp", "ccp", "cfp", "ks", "vs", "cvs", "cbs", "ccs", "cfs")}
    kv_prompt = None
    for l in range(depth):
        final = l == depth - 1
        lp = functools.partial(_layer_param, layer=l)
        mixer_shared = [lp(g_mix), lp(w_in_b, cols=(0, MIX_COLS)), lp(ln_v_g), lp(ln_v_b)]
        mixer_conv = [lp(conv_b_w), lp(conv_c_w), lp(ln_c_g), lp(ln_c_b)]
        merge_params = [lp(g_mix), lp(w_in_b, cols=(MIX_COLS, w_in.shape[2] - MIX_COLS)),
                        lp(w_br_b), lp(w_o_b)]
        ffn_params = [lp(g_ffn), lp(w_up_b), lp(ffn_conv_w), lp(w_down_b), lp(g_ple), lp(w_pg_b),
                      lp(w_ple_b), _whole(g_final.reshape(1, -1))]

        bmat = jnp.repeat(b_sp[l][:, :chunk].T, grp_cols, axis=1)
        (oa, ob, oc, q, k, v, kb, vb, _, nb, nc) = _mixer(
            xp, mixer_shared + [lp(w_sp), _whole(bmat)] + mixer_conv, zeros_b, zeros_c,
            sample=False, tm=tm, sh=1, kv_stack=(l, depth, kv_prompt))
        kv_prompt = (k, v)
        od = _attn_prompt(q, kb, vb, sb_bias[l])
        flat = lambda a: a.reshape(B * S, a.shape[-1])
        x1 = _merge(flat(xp), flat(oa), flat(ob), flat(oc), flat(od), merge_params,
                    tm=tm).reshape(B, S, D)
        p_spec = pl.BlockSpec((None, 1, tm, ple_dim), lambda b, i, l=l: (l, b, i, 0))
        xp, nf = _ffn(x1, (p_prompt, p_spec), zeros_f, ffn_params, tm=tm, sh=1, final=final)
        outs["cbp"].append(nb)
        outs["ccp"].append(nc)
        outs["cfp"].append(nf)

        wv = jnp.repeat(w_sp[l][:, :T, :T].transpose(1, 2, 0).reshape(T * T, n_grp), grp_cols, axis=1)
        bv = jnp.repeat(b_sp[l][:, :T].T, grp_cols, axis=1)
        (oa, ob, oc, q, k, v, _, _, va, nb, nc) = _mixer(
            xs, mixer_shared + [_whole(wv), _whole(bv)] + mixer_conv,
            _token_major(state_conv_b[l]), _token_major(state_conv_c[l]),
            sample=True, tm=T * n_seq, sh=n_seq)
        pad_t = lambda a: jnp.pad(_seq_major(a, n_seq), ((0, 0), (0, tpad - T), (0, 0)))
        od = _attn_sample(pad_t(q), pad_t(k), pad_t(v), sb_bias[l], cache_kt, cache_vt, l, page_table)
        od = _token_major(od[:, :T, :])
        x1 = _merge(xs[0], oa[0], ob[0], oc[0], od[0], merge_params, tm=T * n_seq)[None]
        ps = _token_major(p_sample[l])
        p_spec = pl.BlockSpec((1, T * n_seq, ple_dim), lambda b, i: (b, i, 0))
        xs, nf = _ffn(x1, (ps, p_spec), _token_major(state_ffn_conv[l]), ffn_params,
                      tm=T * n_seq, sh=n_seq, final=final)
        outs["ks"].append(_seq_major(k, n_seq).reshape(n_seq, T, N_HEADS, HEAD_DIM))
        outs["vs"].append(_seq_major(v, n_seq).reshape(n_seq, T, N_HEADS, HEAD_DIM))
        outs["cvs"].append(_seq_major(va, n_seq))
        outs["cbs"].append(_seq_major(nb, n_seq))
        outs["ccs"].append(_seq_major(nc, n_seq))
        outs["cfs"].append(_seq_major(nf, n_seq))

    st = lambda name: jnp.stack(outs[name])
    heads_last = lambda t: t.reshape(depth, B, N_HEADS, HEAD_DIM, S).transpose(0, 1, 4, 2, 3)
    return (xp, _seq_major(xs, n_seq), heads_last(kv_prompt[0]), heads_last(kv_prompt[1]),
            st("cbp"), st("ccp"), st("cfp"),
            st("ks"), st("vs"), st("cvs"), st("cbs"), st("ccs"), st("cfs"))
```

```python
import functools

import jax
import jax.numpy as jnp
from jax import lax
from jax.experimental import pallas as pl
from jax.experimental.pallas import tpu as pltpu

F32 = jnp.float32
BF16 = jnp.bfloat16

EPS = 1e-6
SUBLANES = 8
BR_WIDTH = 256
N_HEADS = 4
HEAD_DIM = 64
MIX_COLS = 10 * BR_WIDTH
CONV_B_TAPS = 3
CONV_C_TAPS = 31
FFN_TAPS = 3
FFN_CHUNK = 256
SB_KEY_BLOCK = 256
PAGES_PER_GROUP = 64
VMEM_LIMIT = 56 * 1024 * 1024


def _rms(x, g):
    return x * lax.rsqrt(jnp.mean(x * x, axis=-1, keepdims=True) + EPS) * g


def _layernorm(x, g, b):
    mu = jnp.mean(x, axis=-1, keepdims=True)
    xc = x - mu
    var = jnp.mean(xc * xc, axis=-1, keepdims=True)
    return xc * lax.rsqrt(var + EPS) * g + b


def _bdot(a, b):
    return jnp.dot(a, b, preferred_element_type=F32)


def _const_spec(shape):
    nd = len(shape)
    return pl.BlockSpec(shape, lambda *_: (0,) * nd, pipeline_mode=pl.Buffered(1))


def _whole(a):
    return a, _const_spec(a.shape)


def _layer_param(a, layer, cols=None):
    if a.ndim == 2:
        a = a[:, None, :]
    if cols is None:
        block = (None,) + a.shape[1:]
        index = (layer,) + (0,) * (a.ndim - 1)
    else:
        block = (pl.Element(1),) + tuple(pl.Element(n) for n in a.shape[1:-1]) + (pl.Element(cols[1]),)
        index = (layer,) + (0,) * (a.ndim - 2) + (cols[0],)
    return a, pl.BlockSpec(block, lambda *_: index, pipeline_mode=pl.Buffered(1))


def _mixer_body(x_ref, gmix_ref, wmix_ref, lnvg_ref, lnvb_ref, wsp_ref, bsp_ref,
                cbw_ref, ccw_ref, lncg_ref, lncb_ref, stb_ref, stc_ref,
                oa_ref, ob_ref, oc_ref, q_ref, k_ref, v_ref, kb_ref, vb_ref,
                va_ref, nb_ref, nc_ref, bbuf, cbuf, wbuf, *, sample, tm, sh, hb, hc, kv_fill):
    i = pl.program_id(1)

    @pl.when(i == 0)
    def _():
        bbuf[0:hb, :] = stb_ref[0]
        cbuf[0:hc, :] = stc_ref[0]

    w = BR_WIDTH
    h = _rms(x_ref[0], gmix_ref[...]).astype(BF16)
    y = _bdot(h, wmix_ref[0])

    ga = jax.nn.gelu(y[:, 0:2 * w])
    u = ga[:, 0:w]
    va = _layernorm(ga[:, w:2 * w], lnvg_ref[...], lnvb_ref[...])
    va_ref[0] = va
    parts = []
    if sample:
        nt = tm // sh
        for t in range(nt):
            s = jnp.broadcast_to(bsp_ref[t:t + 1, :], (sh, w))
            for s2 in range(t + 1):
                s = s + wsp_ref[t * nt + s2:t * nt + s2 + 1, :] * va[s2 * sh:(s2 + 1) * sh, :]
            parts.append(s)
    else:
        ch = wsp_ref.shape[1]
        rr = lax.broadcasted_iota(jnp.int32, (ch, ch), 0)
        cc = lax.broadcasted_iota(jnp.int32, (ch, ch), 1)
        grp = lax.broadcasted_iota(jnp.int32, (ch, w), 1) // (w // wsp_ref.shape[0])
        wg = [jnp.where(cc <= rr, wsp_ref[g], 0.0).astype(BF16) for g in range(wsp_ref.shape[0])]
        for c in range(tm // ch):
            vc = va[c * ch:(c + 1) * ch, :]
            s = bsp_ref[...]
            for g in range(wsp_ref.shape[0]):
                s = s + _bdot(wg[g], jnp.where(grp == g, vc, 0.0).astype(BF16))
            parts.append(s)
    oa_ref[0] = u * jnp.concatenate(parts, axis=0)

    bbuf[hb:hb + tm, :] = y[:, 3 * w:4 * w] * y[:, 4 * w:5 * w]
    base = hb - (CONV_B_TAPS - 1) * sh
    cb = cbw_ref[0:1, :] * bbuf[base:base + tm, :]
    for j in range(1, CONV_B_TAPS):
        cb = cb + cbw_ref[j:j + 1, :] * bbuf[base + j * sh:base + j * sh + tm, :]
    ob_ref[0] = y[:, 2 * w:3 * w] * cb
    nb_ref[0] = bbuf[hb + tm - (CONV_B_TAPS - 1) * sh:hb + tm, :]
    bbuf[0:hb, :] = bbuf[tm:tm + hb, :]

    cbuf[hc:hc + tm, :] = y[:, 5 * w:6 * w] * jax.nn.sigmoid(y[:, 6 * w:7 * w])
    base = hc - (CONV_C_TAPS - 1) * sh
    cv = None
    residues = SUBLANES if sh % SUBLANES else 1
    for b in range(residues):
        taps = list(range(b, CONV_C_TAPS, residues))
        span = (taps[-1] - b) * sh
        start = base + b * sh
        if start % SUBLANES:
            win = wbuf.at[b]
            win[0:span + tm, :] = cbuf[start:start + span + tm, :]
            start = 0
        else:
            win = cbuf
        for j in taps:
            off = start + (j - b) * sh
            term = ccw_ref[j:j + 1, :] * win[off:off + tm, :]
            cv = term if cv is None else cv + term
    oc_ref[0] = jax.nn.silu(_layernorm(cv, lncg_ref[...], lncb_ref[...]))
    nc_ref[0] = cbuf[hc + tm - (CONV_C_TAPS - 1) * sh:hc + tm, :]
    cbuf[0:hc, :] = cbuf[tm:tm + hc, :]

    q = y[:, 7 * w:8 * w]
    k = y[:, 8 * w:9 * w]
    v = y[:, 9 * w:10 * w]
    q_ref[0] = q
    if sample:
        k_ref[0] = k
        v_ref[0] = v
    else:
        k_ref[0, 0] = k.T
        v_ref[0, 0] = v.T
        for d in range(1, 1 + kv_fill):
            k_ref[d, 0] = jnp.zeros((w, tm), F32)
            v_ref[d, 0] = jnp.zeros((w, tm), F32)
    kb_ref[0] = k.astype(BF16)
    vb_ref[0] = v.astype(BF16)


def _without_refs(body, at, n):
    def wrapped(*refs, **kwargs):
        return body(*refs[:at], *refs[at + n:], **kwargs)
    return wrapped


def _mixer(x, params, st_b, st_c, *, sample, tm, sh, kv_stack=None):
    B, S, D = x.shape
    w = BR_WIDTH
    hb, hc = st_b.shape[1], st_c.shape[1]
    nt = S // tm
    tok = lambda width: pl.BlockSpec((1, tm, width), lambda b, i: (b, i, 0))
    per_seq = lambda rows, width: pl.BlockSpec((1, rows, width), lambda b, i: (b, 0, 0))
    consts = [a for a, _ in params]
    f32_tok = jax.ShapeDtypeStruct((B, S, w), F32)
    bf_tok = jax.ShapeDtypeStruct((B, S, w), BF16)
    realign_shape = (SUBLANES, tm + CONV_C_TAPS - 1, w) if sh % SUBLANES else (1, SUBLANES, w)
    stacks, stack_specs, aliases, kv_fill = [], [], {}, 0
    if sample:
        kv_spec, kv_shape = tok(w), f32_tok
    else:
        layer, depth, previous = kv_stack
        kv_shape = jax.ShapeDtypeStruct((depth, B, w, S), F32)
        if previous is None:
            assert layer == 0
            kv_fill = depth - 1
            kv_spec = pl.BlockSpec((depth, 1, w, tm), lambda b, i: (0, b, 0, i))
        else:
            kv_spec = pl.BlockSpec((1, 1, w, tm), lambda b, i: (layer, b, 0, i))
            n_in = 1 + len(consts) + 2
            stacks, stack_specs = list(previous), [pl.BlockSpec(memory_space=pl.ANY)] * 2
            aliases = {n_in: 4, n_in + 1: 5}
    body = functools.partial(_mixer_body, sample=sample, tm=tm, sh=sh, hb=hb, hc=hc, kv_fill=kv_fill)
    if stacks:
        body = _without_refs(body, n_in, 2)
    nb_rows, nc_rows = (CONV_B_TAPS - 1) * sh, (CONV_C_TAPS - 1) * sh
    return pl.pallas_call(
        body,
        grid=(B, nt),
        in_specs=[tok(D)] + [spec for _, spec in params] + [per_seq(hb, w), per_seq(hc, w)]
                 + stack_specs,
        input_output_aliases=aliases,
        out_specs=[tok(w)] * 4 + [kv_spec] * 2 + [tok(w)] * 3
                  + [per_seq(nb_rows, w), per_seq(nc_rows, w)],
        out_shape=[f32_tok] * 4 + [kv_shape] * 2 + [bf_tok] * 2 + [f32_tok]
                  + [jax.ShapeDtypeStruct((B, nb_rows, w), F32),
                     jax.ShapeDtypeStruct((B, nc_rows, w), F32)],
        scratch_shapes=[pltpu.VMEM((hb + tm, w), F32), pltpu.VMEM((hc + tm, w), F32),
                        pltpu.VMEM(realign_shape, F32)],
        compiler_params=pltpu.CompilerParams(
            dimension_semantics=("arbitrary", "arbitrary"), vmem_limit_bytes=VMEM_LIMIT),
        name="mixer_sample" if sample else "mixer_prompt",
    )(x, *consts, st_b, st_c, *stacks)


def _suffix_matrix():
    n = SB_KEY_BLOCK
    r = lax.broadcasted_iota(jnp.int32, (n, n), 0)
    c = lax.broadcasted_iota(jnp.int32, (n, n), 1)
    return (r > c).astype(BF16)


_NT = (((1,), (1,)), ((), ()))


def _neg_abs(x):
    return pltpu.bitcast(pltpu.bitcast(x, jnp.uint32) | jnp.uint32(0x80000000), F32)


def _sb_tile(qm, kt, vt, bias_col, run, suffix, mask, *, keys_in_lanes, stack_blocks=False):
    if keys_in_lanes:
        z = _bdot(qm.astype(kt.dtype), kt) + bias_col
    else:
        z = lax.dot_general(qm.astype(kt.dtype), kt, _NT, preferred_element_type=F32) + bias_col
    sp = jnp.maximum(z, 0.0) + jnp.log(1.0 + jnp.exp(_neg_abs(z)))
    if mask is not None:
        sp = jnp.where(mask, sp, 0.0)
    m = z.shape[0]
    nblk = z.shape[1] // SB_KEY_BLOCK
    blocks = [sp[:, b * SB_KEY_BLOCK:(b + 1) * SB_KEY_BLOCK] for b in range(nblk)]
    if stack_blocks and nblk > 1:
        stacked = _bdot(jnp.concatenate(blocks, axis=0).astype(BF16), suffix)
        within = [stacked[b * m:(b + 1) * m, :] for b in range(nblk)]
    else:
        within = [_bdot(blk.astype(BF16), suffix) for blk in blocks]
    later = [None] * nblk
    for b in reversed(range(nblk)):
        later[b] = within[b] + run
        run = run + (within[b][:, 0:1] + blocks[b][:, 0:1])
    later = later[0] if nblk == 1 else jnp.concatenate(later, axis=1)
    a = jnp.exp(z - sp - later)
    if mask is not None:
        a = jnp.where(mask, a, 0.0)
    a = a.astype(BF16).astype(vt.dtype)
    if keys_in_lanes:
        return lax.dot_general(a, vt, _NT, preferred_element_type=F32), run
    return _bdot(a, vt), run


def _head_stack(q, qm_ref, rows):
    lane_head = lax.broadcasted_iota(jnp.int32, q.shape, 1) // HEAD_DIM
    for h in range(N_HEADS):
        qm_ref[h * rows:(h + 1) * rows, :] = jnp.where(lane_head == h, q, 0.0).astype(BF16)


def _head_unstack(acc_ref, rows):
    lane_head = lax.broadcasted_iota(jnp.int32, (rows, N_HEADS * HEAD_DIM), 1) // HEAD_DIM
    out = jnp.where(lane_head == 0, acc_ref[0:rows, :], 0.0)
    for h in range(1, N_HEADS):
        out = out + jnp.where(lane_head == h, acc_ref[h * rows:(h + 1) * rows, :], 0.0)
    return out


def _attend_body(pt_ref, q_ref, kb_ref, vb_ref, bias_ref, qs_ref, kn_ref, vn_ref, bias_s_ref,
                 ck_hbm, cv_hbm, o_ref, os_ref, qm_ref, acc_ref, run_ref, qms_ref, kpg, vpg, sems,
                 *, tq, wide, layer, n_pages, npg, nsq, tpad):
    i = pl.program_id(1)
    step_idx = pl.program_id(0) * pl.num_programs(1) + i
    n_groups = n_pages // npg
    blk = SB_KEY_BLOCK
    suffix = _suffix_matrix()

    def page_copies(j, g):
        seq = step_idx * nsq + j
        copies = []
        for p in range(npg):
            page = pt_ref[seq * n_pages + g * npg + p]
            slot = (j * n_groups + g) * npg + p
            copies.append(pltpu.make_async_copy(ck_hbm.at[layer, page], kpg.at[slot],
                                                sems.at[0, j * n_groups + g]))
            copies.append(pltpu.make_async_copy(cv_hbm.at[layer, page], vpg.at[slot],
                                                sems.at[1, j * n_groups + g]))
        return copies

    for j in range(nsq):
        for g in range(n_groups):
            for copy in page_copies(j, g):
                copy.start()

    m = N_HEADS * tq
    _head_stack(q_ref[0] * (HEAD_DIM ** -0.5), qm_ref, tq)
    qm = qm_ref[...]
    bias_col = bias_ref[...]

    def keys(start, n):
        start = pl.multiple_of(start, blk)
        return kb_ref[0, pl.ds(start, n), :], vb_ref[0, pl.ds(start, n), :]

    t_q = lax.broadcasted_iota(jnp.int32, (m, tq), 0) & (tq - 1)
    t_k = lax.broadcasted_iota(jnp.int32, (m, tq), 1)
    pv, run = _sb_tile(qm, *keys(i * tq, tq), bias_col, jnp.zeros((m, 1), F32), suffix,
                       t_k < t_q, keys_in_lanes=False)
    acc_ref[...] = pv
    run_ref[...] = run

    def step(start, n):
        pv, run = _sb_tile(qm, *keys(start, n), bias_col, run_ref[...], suffix, None,
                           keys_in_lanes=False)
        acc_ref[...] += pv
        run_ref[...] = run

    per_wide = wide // blk
    jd = (i * tq) // blk
    n_single = jd % per_wide
    n_wide = jd // per_wide

    def single_body(n, carry):
        step((jd - 1 - n) * blk, blk)
        return carry

    def wide_body(n, carry):
        step((n_wide - 1 - n) * wide, wide)
        return carry

    lax.fori_loop(0, n_single, single_body, 0)
    lax.fori_loop(0, n_wide, wide_body, 0)
    o_ref[0] = _head_unstack(acc_ref, tq)

    ms = N_HEADS * tpad
    bias_s = bias_s_ref[...]
    s_q = lax.broadcasted_iota(jnp.int32, (ms, blk), 0) & (tpad - 1)
    s_k = lax.broadcasted_iota(jnp.int32, (ms, blk), 1)
    for j in range(nsq):
        for g in range(n_groups):
            for copy in page_copies(j, g):
                copy.wait()
    for j in range(nsq):
        qms = qms_ref.at[j * ms:(j + 1) * ms]
        _head_stack(qs_ref[j] * (HEAD_DIM ** -0.5), qms, tpad)
        kn = jnp.concatenate([kn_ref[j], jnp.zeros((blk - tpad, kn_ref.shape[2]), F32)], axis=0)
        vn = jnp.concatenate([vn_ref[j], jnp.zeros((blk - tpad, vn_ref.shape[2]), F32)], axis=0)
        acc, run = _sb_tile(qms[...], kn.astype(BF16), vn.astype(BF16), bias_s,
                            jnp.zeros((ms, 1), F32), suffix, s_k < s_q, keys_in_lanes=False)
        for g in reversed(range(n_groups)):
            first = (j * n_groups + g) * npg
            kt = jnp.concatenate([kpg[first + p] for p in range(npg)], axis=1)
            vt = jnp.concatenate([vpg[first + p] for p in range(npg)], axis=1)
            pv, run = _sb_tile(qms[...], kt, vt, bias_s, run, suffix, None,
                               keys_in_lanes=True, stack_blocks=True)
            acc = acc + pv
        os_ref[j] = _head_unstack(acc, tpad)


def _attend(q, kb, vb, qs, k_new, v_new, sb_bias, cache_kt, cache_vt, layer, page_table,
            *, tq=2 * SB_KEY_BLOCK, wide=2 * SB_KEY_BLOCK):
    B, S, w = q.shape
    n_seq, tpad, _ = qs.shape
    page = cache_kt.shape[3]
    n_pages = page_table.shape[1]
    tq = min(tq, S)
    nq = S // tq
    npg = min(PAGES_PER_GROUP, n_pages)
    assert tq % SB_KEY_BLOCK == 0 and tq & (tq - 1) == 0 and S % tq == 0
    assert n_seq % (B * nq) == 0 and n_pages % npg == 0 and (npg * page) % SB_KEY_BLOCK == 0
    nsq = n_seq // (B * nq)
    m, ms = N_HEADS * tq, N_HEADS * tpad
    n_slots = nsq * n_pages
    bias_rows = jnp.repeat(sb_bias.astype(F32), tq).reshape(m, 1)
    bias_rows_s = jnp.repeat(sb_bias.astype(F32), tpad).reshape(ms, 1)
    tile = pl.BlockSpec((1, tq, w), lambda b, i, pt: (b, i, 0))
    whole_seq = pl.BlockSpec((1, S, w), lambda b, i, pt: (b, 0, 0), pipeline_mode=pl.Buffered(1))
    per_step = pl.BlockSpec((nsq, tpad, w), lambda b, i, pt: (b * nq + i, 0, 0))
    const = lambda rows: pl.BlockSpec((rows, 1), lambda b, i, pt: (0, 0), pipeline_mode=pl.Buffered(1))
    hbm = pl.BlockSpec(memory_space=pl.ANY)
    grid_spec = pltpu.PrefetchScalarGridSpec(
        num_scalar_prefetch=1,
        grid=(B, nq),
        in_specs=[tile, whole_seq, whole_seq, const(m), per_step, per_step, per_step, const(ms),
                  hbm, hbm],
        out_specs=[tile, per_step],
        scratch_shapes=[pltpu.VMEM((m, w), BF16), pltpu.VMEM((m, w), F32), pltpu.VMEM((m, 1), F32),
                        pltpu.VMEM((nsq * ms, w), BF16),
                        pltpu.VMEM((n_slots, w, page), F32), pltpu.VMEM((n_slots, w, page), F32),
                        pltpu.SemaphoreType.DMA((2, nsq * (n_pages // npg)))],
    )
    return pl.pallas_call(
        functools.partial(_attend_body, tq=tq, wide=wide, layer=layer, n_pages=n_pages, npg=npg,
                          nsq=nsq, tpad=tpad),
        grid_spec=grid_spec,
        out_shape=[jax.ShapeDtypeStruct((B, S, w), F32), jax.ShapeDtypeStruct((n_seq, tpad, w), F32)],
        compiler_params=pltpu.CompilerParams(
            dimension_semantics=("arbitrary", "arbitrary"), vmem_limit_bytes=VMEM_LIMIT),
        name="attend",
    )(page_table.reshape(-1), q, kb, vb, bias_rows, qs, k_new, v_new, bias_rows_s, cache_kt, cache_vt)


def _merge_body(x_ref, oa_ref, ob_ref, oc_ref, od_ref, gmix_ref, wgate_ref, wbr_ref, wo_ref, xo_ref):
    x = x_ref[...]
    d = x.shape[1]
    h = _rms(x, gmix_ref[...]).astype(BF16)
    merged = None
    for kk, br_ref in enumerate((oa_ref, ob_ref, oc_ref, od_ref)):
        gate = jax.nn.sigmoid(_bdot(h, wgate_ref[0, :, kk * d:(kk + 1) * d]))
        term = gate * _bdot(br_ref[...].astype(BF16), wbr_ref[kk])
        merged = term if merged is None else merged + term
    xo_ref[...] = x + _bdot(merged.astype(BF16), wo_ref[...])


def _merge(x, oa, ob, oc, od, params, *, tm):
    n, d = x.shape
    w = BR_WIDTH
    tok = lambda width: pl.BlockSpec((tm, width), lambda i: (i, 0))
    consts = [a for a, _ in params]
    return pl.pallas_call(
        _merge_body,
        grid=(n // tm,),
        in_specs=[tok(d)] + [tok(w)] * 4 + [spec for _, spec in params],
        out_specs=tok(d),
        out_shape=jax.ShapeDtypeStruct((n, d), F32),
        compiler_params=pltpu.CompilerParams(
            dimension_semantics=("arbitrary",), vmem_limit_bytes=VMEM_LIMIT),
        name="merge",
    )(x, oa, ob, oc, od, *consts)


def _ffn_body(x_ref, p_ref, st_ref, gffn_ref, wup_ref, cw_ref, wdn_ref, gple_ref, wpg_ref,
              wple_ref, gfin_ref, xo_ref, nf_ref, ubuf, act_ref, *, tm, sh, hh, final):
    i = pl.program_id(1)

    @pl.when(i == 0)
    def _():
        ubuf[0:hh, :] = st_ref[0]

    x = x_ref[0]
    hn = _rms(x, gffn_ref[...]).astype(BF16)
    d_ff = wdn_ref.shape[0]
    cw = FFN_CHUNK
    ubuf[hh:hh + tm, :] = _bdot(hn, wup_ref[...])
    base = hh - (FFN_TAPS - 1) * sh
    for c in range(d_ff // cw):
        conv = []
        for col in (c * cw, d_ff + c * cw):
            cv = cw_ref[0:1, col:col + cw] * ubuf[base:base + tm, col:col + cw]
            for j in range(1, FFN_TAPS):
                cv = cv + cw_ref[j:j + 1, col:col + cw] * ubuf[base + j * sh:base + j * sh + tm, col:col + cw]
            conv.append(cv)
        act_ref[:, c * cw:(c + 1) * cw] = (jax.nn.silu(conv[0]) * conv[1]).astype(BF16)
    nf_ref[0] = ubuf[hh + tm - (FFN_TAPS - 1) * sh:hh + tm, :]
    ubuf[0:hh, :] = ubuf[tm:tm + hh, :]
    x = x + _bdot(act_ref[...], wdn_ref[...])
    gate = jax.nn.sigmoid(_bdot(_rms(x, gple_ref[...]).astype(BF16), wpg_ref[...]))
    x = x + gate * _bdot(p_ref[0].astype(BF16), wple_ref[...])
    xo_ref[0] = _rms(x, gfin_ref[...]) if final else x


def _ffn(x, p, st_f, params, *, tm, sh, final):
    B, S, D = x.shape
    hh, up_cols = st_f.shape[1], st_f.shape[2]
    tok = lambda width: pl.BlockSpec((1, tm, width), lambda b, i: (b, i, 0))
    consts = [a for a, _ in params]
    p, p_spec = p
    nf_rows = (FFN_TAPS - 1) * sh
    return pl.pallas_call(
        functools.partial(_ffn_body, tm=tm, sh=sh, hh=hh, final=final),
        grid=(B, S // tm),
        in_specs=[tok(D), p_spec, pl.BlockSpec((1, hh, up_cols), lambda b, i: (b, 0, 0))]
                 + [spec for _, spec in params],
        out_specs=[tok(D), pl.BlockSpec((1, nf_rows, up_cols), lambda b, i: (b, 0, 0))],
        out_shape=[jax.ShapeDtypeStruct((B, S, D), F32),
                   jax.ShapeDtypeStruct((B, nf_rows, up_cols), F32)],
        scratch_shapes=[pltpu.VMEM((hh + tm, up_cols), F32), pltpu.VMEM((tm, up_cols // 2), BF16)],
        compiler_params=pltpu.CompilerParams(
            dimension_semantics=("arbitrary", "arbitrary"), vmem_limit_bytes=VMEM_LIMIT),
        name="ffn",
    )(x, p, st_f, *consts)


def _token_major(a):
    n_seq, t, c = a.shape
    return a.transpose(1, 0, 2).reshape(1, t * n_seq, c)


def _seq_major(a, n_seq):
    c = a.shape[-1]
    return a.reshape(-1, n_seq, c).transpose(1, 0, 2)


def kernel(x_prompt, x_sample, p_prompt, p_sample, cache_k, cache_v, page_table, state_conv_b, state_conv_c, state_ffn_conv, g_mix, w_in, ln_v_g, ln_v_b, w_sp, b_sp, conv_b_w, conv_c_w, ln_c_g, ln_c_b, sb_bias, w_br, w_o, g_ffn, w_up, ffn_conv_w, w_down, g_ple, w_ple_gate, w_ple, g_final):
    depth = w_in.shape[0]
    B, S, D = x_prompt.shape
    n_seq, T, _ = x_sample.shape
    w = BR_WIDTH
    n_grp = w_sp.shape[1]
    chunk = w_sp.shape[2]
    tm = min(512, S)
    tpad = 8
    assert S % tm == 0 and tm % chunk == 0 and S % SB_KEY_BLOCK == 0 and T <= tpad

    xp = x_prompt
    xs = _token_major(x_sample)
    zeros_b = jnp.zeros((B, 8, w), F32)
    zeros_c = jnp.zeros((B, 32, w), F32)
    zeros_f = jnp.zeros((B, 8, w_up.shape[2]), F32)
    n_phys, page = cache_k.shape[1], cache_k.shape[2]
    cache_kt = cache_k.transpose(0, 1, 3, 4, 2).reshape(depth, n_phys, w, page)
    cache_vt = cache_v.transpose(0, 1, 3, 4, 2).reshape(depth, n_phys, w, page)

    w_in_b = w_in.astype(BF16)
    w_br_b = w_br.astype(BF16)
    w_o_b = w_o.astype(BF16)
    w_up_b = w_up.astype(BF16)
    w_down_b = w_down.astype(BF16)
    w_pg_b = w_ple_gate.astype(BF16)
    w_ple_b = w_ple.astype(BF16)
    grp_cols = w // n_grp
    ple_dim = p_prompt.shape[-1]

    outs = {name: [] for name in ("cbp", "ccp", "cfp", "ks", "vs", "cvs", "cbs", "ccs", "cfs")}
    kv_prompt = None
    for l in range(depth):
        final = l == depth - 1
        lp = functools.partial(_layer_param, layer=l)
        mixer_shared = [lp(g_mix), lp(w_in_b, cols=(0, MIX_COLS)), lp(ln_v_g), lp(ln_v_b)]
        mixer_conv = [lp(conv_b_w), lp(conv_c_w), lp(ln_c_g), lp(ln_c_b)]
        merge_params = [lp(g_mix), lp(w_in_b, cols=(MIX_COLS, w_in.shape[2] - MIX_COLS)),
                        lp(w_br_b), lp(w_o_b)]
        ffn_params = [lp(g_ffn), lp(w_up_b), lp(ffn_conv_w), lp(w_down_b), lp(g_ple), lp(w_pg_b),
                      lp(w_ple_b), _whole(g_final.reshape(1, -1))]

        bmat = jnp.repeat(b_sp[l][:, :chunk].T, grp_cols, axis=1)
        (oa, ob, oc, q, k, v, kb, vb, _, nb, nc) = _mixer(
            xp, mixer_shared + [lp(w_sp), _whole(bmat)] + mixer_conv, zeros_b, zeros_c,
            sample=False, tm=tm, sh=1, kv_stack=(l, depth, kv_prompt))
        kv_prompt = (k, v)
        outs["cbp"].append(nb)
        outs["ccp"].append(nc)

        wv = jnp.repeat(w_sp[l][:, :T, :T].transpose(1, 2, 0).reshape(T * T, n_grp), grp_cols, axis=1)
        bv = jnp.repeat(b_sp[l][:, :T].T, grp_cols, axis=1)
        (oa_s, ob_s, oc_s, q_s, k, v, _, _, va, nb, nc) = _mixer(
            xs, mixer_shared + [_whole(wv), _whole(bv)] + mixer_conv,
            _token_major(state_conv_b[l]), _token_major(state_conv_c[l]),
            sample=True, tm=T * n_seq, sh=n_seq)

        pad_t = lambda a: jnp.pad(_seq_major(a, n_seq), ((0, 0), (0, tpad - T), (0, 0)))
        od, od_s = _attend(q, kb, vb, pad_t(q_s), pad_t(k), pad_t(v), sb_bias[l], cache_kt, cache_vt,
                           l, page_table)
        od_s = _token_major(od_s[:, :T, :])

        flat = lambda a: a.reshape(B * S, a.shape[-1])
        x1 = _merge(flat(xp), flat(oa), flat(ob), flat(oc), flat(od), merge_params,
                    tm=tm).reshape(B, S, D)
        p_spec = pl.BlockSpec((None, 1, tm, ple_dim), lambda b, i, l=l: (l, b, i, 0))
        xp, nf = _ffn(x1, (p_prompt, p_spec), zeros_f, ffn_params, tm=tm, sh=1, final=final)
        outs["cfp"].append(nf)

        x1 = _merge(xs[0], oa_s[0], ob_s[0], oc_s[0], od_s[0], merge_params, tm=T * n_seq)[None]
        ps = _token_major(p_sample[l])
        p_spec = pl.BlockSpec((1, T * n_seq, ple_dim), lambda b, i: (b, i, 0))
        xs, nf = _ffn(x1, (ps, p_spec), _token_major(state_ffn_conv[l]), ffn_params,
                      tm=T * n_seq, sh=n_seq, final=final)
        outs["ks"].append(_seq_major(k, n_seq).reshape(n_seq, T, N_HEADS, HEAD_DIM))
        outs["vs"].append(_seq_major(v, n_seq).reshape(n_seq, T, N_HEADS, HEAD_DIM))
        outs["cvs"].append(_seq_major(va, n_seq))
        outs["cbs"].append(_seq_major(nb, n_seq))
        outs["ccs"].append(_seq_major(nc, n_seq))
        outs["cfs"].append(_seq_major(nf, n_seq))

    st = lambda name: jnp.stack(outs[name])
    heads_last = lambda t: t.reshape(depth, B, N_HEADS, HEAD_DIM, S).transpose(0, 1, 4, 2, 3)
    return (xp, _seq_major(xs, n_seq), heads_last(kv_prompt[0]), heads_last(kv_prompt[1]),
            st("cbp"), st("ccp"), st("cfp"),
            st("ks"), st("vs"), st("cvs"), st("cbs"), st("ccs"), st("cfs"))
```

```python
import functools

import jax
import jax.numpy as jnp
from jax import lax
from jax.experimental import pallas as pl
from jax.experimental.pallas import tpu as pltpu

F32 = jnp.float32
BF16 = jnp.bfloat16

EPS = 1e-6
SUBLANES = 8
BR_WIDTH = 256
N_HEADS = 4
HEAD_DIM = 64
MIX_COLS = 10 * BR_WIDTH
CONV_B_TAPS = 3
CONV_C_TAPS = 31
FFN_TAPS = 3
FFN_CHUNK = 256
TOKEN_TILE = 512
SB_KEY_BLOCK = 256
PAGES_PER_GROUP = 64
VMEM_LIMIT = 56 * 1024 * 1024


def _rms(x, g):
    return x * lax.rsqrt(jnp.mean(x * x, axis=-1, keepdims=True) + EPS) * g


def _layernorm(x, g, b):
    mu = jnp.mean(x, axis=-1, keepdims=True)
    xc = x - mu
    var = jnp.mean(xc * xc, axis=-1, keepdims=True)
    return xc * lax.rsqrt(var + EPS) * g + b


def _bdot(a, b):
    return jnp.dot(a, b, preferred_element_type=F32)


def _const_spec(shape):
    nd = len(shape)
    return pl.BlockSpec(shape, lambda *_: (0,) * nd, pipeline_mode=pl.Buffered(1))


def _whole(a):
    return a, _const_spec(a.shape)


def _layer_param(a, layer, cols=None):
    if a.ndim == 2:
        a = a[:, None, :]
    if cols is None:
        block = (None,) + a.shape[1:]
        index = (layer,) + (0,) * (a.ndim - 1)
    else:
        block = (pl.Element(1),) + tuple(pl.Element(n) for n in a.shape[1:-1]) + (pl.Element(cols[1]),)
        index = (layer,) + (0,) * (a.ndim - 2) + (cols[0],)
    return a, pl.BlockSpec(block, lambda *_: index, pipeline_mode=pl.Buffered(1))


def _mixer_body(x_ref, gmix_ref, wmix_ref, lnvg_ref, lnvb_ref, wsp_ref, bsp_ref,
                cbw_ref, ccw_ref, lncg_ref, lncb_ref, stb_ref, stc_ref,
                oa_ref, ob_ref, oc_ref, q_ref, k_ref, v_ref, kb_ref, vb_ref,
                va_ref, nb_ref, nc_ref, bbuf, cbuf, wbuf, *, sample, tm, sh, hb, hc, kv_fill):
    i = pl.program_id(1)

    @pl.when(i == 0)
    def _():
        bbuf[0:hb, :] = stb_ref[0]
        cbuf[0:hc, :] = stc_ref[0]

    w = BR_WIDTH
    h = _rms(x_ref[0], gmix_ref[...]).astype(BF16)
    y = _bdot(h, wmix_ref[0])

    ga = jax.nn.gelu(y[:, 0:2 * w])
    u = ga[:, 0:w]
    va = _layernorm(ga[:, w:2 * w], lnvg_ref[...], lnvb_ref[...])
    va_ref[0] = va
    parts = []
    if sample:
        nt = tm // sh
        for t in range(nt):
            s = jnp.broadcast_to(bsp_ref[t:t + 1, :], (sh, w))
            for s2 in range(t + 1):
                s = s + wsp_ref[t * nt + s2:t * nt + s2 + 1, :] * va[s2 * sh:(s2 + 1) * sh, :]
            parts.append(s)
    else:
        ch = wsp_ref.shape[1]
        rr = lax.broadcasted_iota(jnp.int32, (ch, ch), 0)
        cc = lax.broadcasted_iota(jnp.int32, (ch, ch), 1)
        grp = lax.broadcasted_iota(jnp.int32, (ch, w), 1) // (w // wsp_ref.shape[0])
        wg = [jnp.where(cc <= rr, wsp_ref[g], 0.0).astype(BF16) for g in range(wsp_ref.shape[0])]
        for c in range(tm // ch):
            vc = va[c * ch:(c + 1) * ch, :]
            s = bsp_ref[...]
            for g in range(wsp_ref.shape[0]):
                s = s + _bdot(wg[g], jnp.where(grp == g, vc, 0.0).astype(BF16))
            parts.append(s)
    oa_ref[0] = u * jnp.concatenate(parts, axis=0)

    bbuf[hb:hb + tm, :] = y[:, 3 * w:4 * w] * y[:, 4 * w:5 * w]
    base = hb - (CONV_B_TAPS - 1) * sh
    cb = cbw_ref[0:1, :] * bbuf[base:base + tm, :]
    for j in range(1, CONV_B_TAPS):
        cb = cb + cbw_ref[j:j + 1, :] * bbuf[base + j * sh:base + j * sh + tm, :]
    ob_ref[0] = y[:, 2 * w:3 * w] * cb
    nb_ref[0] = bbuf[hb + tm - (CONV_B_TAPS - 1) * sh:hb + tm, :]
    bbuf[0:hb, :] = bbuf[tm:tm + hb, :]

    cbuf[hc:hc + tm, :] = y[:, 5 * w:6 * w] * jax.nn.sigmoid(y[:, 6 * w:7 * w])
    base = hc - (CONV_C_TAPS - 1) * sh
    cv = None
    residues = SUBLANES if sh % SUBLANES else 1
    for b in range(residues):
        taps = list(range(b, CONV_C_TAPS, residues))
        span = (taps[-1] - b) * sh
        start = base + b * sh
        if start % SUBLANES:
            win = wbuf.at[b]
            win[0:span + tm, :] = cbuf[start:start + span + tm, :]
            start = 0
        else:
            win = cbuf
        for j in taps:
            off = start + (j - b) * sh
            term = ccw_ref[j:j + 1, :] * win[off:off + tm, :]
            cv = term if cv is None else cv + term
    oc_ref[0] = jax.nn.silu(_layernorm(cv, lncg_ref[...], lncb_ref[...]))
    nc_ref[0] = cbuf[hc + tm - (CONV_C_TAPS - 1) * sh:hc + tm, :]
    cbuf[0:hc, :] = cbuf[tm:tm + hc, :]

    q = y[:, 7 * w:8 * w]
    k = y[:, 8 * w:9 * w]
    v = y[:, 9 * w:10 * w]
    q_ref[0] = q
    if sample:
        k_ref[0] = k
        v_ref[0] = v
    else:
        k_ref[0, 0] = k.T
        v_ref[0, 0] = v.T
        for d in range(1, 1 + kv_fill):
            k_ref[d, 0] = jnp.zeros((w, tm), F32)
            v_ref[d, 0] = jnp.zeros((w, tm), F32)
    kb_ref[0] = k.astype(BF16)
    vb_ref[0] = v.astype(BF16)


def _without_refs(body, at, n):
    def wrapped(*refs, **kwargs):
        return body(*refs[:at], *refs[at + n:], **kwargs)
    return wrapped


def _mixer(x, params, st_b, st_c, *, sample, tm, sh, kv_stack=None):
    B, S, D = x.shape
    w = BR_WIDTH
    hb, hc = st_b.shape[1], st_c.shape[1]
    nt = S // tm
    tok = lambda width: pl.BlockSpec((1, tm, width), lambda b, i: (b, i, 0))
    per_seq = lambda rows, width: pl.BlockSpec((1, rows, width), lambda b, i: (b, 0, 0))
    consts = [a for a, _ in params]
    f32_tok = jax.ShapeDtypeStruct((B, S, w), F32)
    bf_tok = jax.ShapeDtypeStruct((B, S, w), BF16)
    realign_shape = (SUBLANES, tm + CONV_C_TAPS - 1, w) if sh % SUBLANES else (1, SUBLANES, w)
    stacks, stack_specs, aliases, kv_fill = [], [], {}, 0
    if sample:
        kv_spec, kv_shape = tok(w), f32_tok
    else:
        layer, depth, previous = kv_stack
        kv_shape = jax.ShapeDtypeStruct((depth, B, w, S), F32)
        if previous is None:
            assert layer == 0
            kv_fill = depth - 1
            kv_spec = pl.BlockSpec((depth, 1, w, tm), lambda b, i: (0, b, 0, i))
        else:
            kv_spec = pl.BlockSpec((1, 1, w, tm), lambda b, i: (layer, b, 0, i))
            n_in = 1 + len(consts) + 2
            stacks, stack_specs = list(previous), [pl.BlockSpec(memory_space=pl.ANY)] * 2
            aliases = {n_in: 4, n_in + 1: 5}
    body = functools.partial(_mixer_body, sample=sample, tm=tm, sh=sh, hb=hb, hc=hc, kv_fill=kv_fill)
    if stacks:
        body = _without_refs(body, n_in, 2)
    nb_rows, nc_rows = (CONV_B_TAPS - 1) * sh, (CONV_C_TAPS - 1) * sh
    return pl.pallas_call(
        body,
        grid=(B, nt),
        in_specs=[tok(D)] + [spec for _, spec in params] + [per_seq(hb, w), per_seq(hc, w)]
                 + stack_specs,
        input_output_aliases=aliases,
        out_specs=[tok(w)] * 4 + [kv_spec] * 2 + [tok(w)] * 3
                  + [per_seq(nb_rows, w), per_seq(nc_rows, w)],
        out_shape=[f32_tok] * 4 + [kv_shape] * 2 + [bf_tok] * 2 + [f32_tok]
                  + [jax.ShapeDtypeStruct((B, nb_rows, w), F32),
                     jax.ShapeDtypeStruct((B, nc_rows, w), F32)],
        scratch_shapes=[pltpu.VMEM((hb + tm, w), F32), pltpu.VMEM((hc + tm, w), F32),
                        pltpu.VMEM(realign_shape, F32)],
        compiler_params=pltpu.CompilerParams(
            dimension_semantics=("arbitrary", "arbitrary"), vmem_limit_bytes=VMEM_LIMIT),
        name="mixer_sample" if sample else "mixer_prompt",
    )(x, *consts, st_b, st_c, *stacks)


def _suffix_matrix():
    n = SB_KEY_BLOCK
    r = lax.broadcasted_iota(jnp.int32, (n, n), 0)
    c = lax.broadcasted_iota(jnp.int32, (n, n), 1)
    return (r > c).astype(BF16)


_NT = (((1,), (1,)), ((), ()))


def _neg_abs(x):
    return pltpu.bitcast(pltpu.bitcast(x, jnp.uint32) | jnp.uint32(0x80000000), F32)


def _sb_tile(qm, kt, vt, bias_col, run, suffix, mask, *, keys_in_lanes, stack_blocks=False):
    if keys_in_lanes:
        z = _bdot(qm.astype(kt.dtype), kt) + bias_col
    else:
        z = lax.dot_general(qm.astype(kt.dtype), kt, _NT, preferred_element_type=F32) + bias_col
    sp = jnp.maximum(z, 0.0) + jnp.log(1.0 + jnp.exp(_neg_abs(z)))
    if mask is not None:
        sp = jnp.where(mask, sp, 0.0)
    m = z.shape[0]
    nblk = z.shape[1] // SB_KEY_BLOCK
    blocks = [sp[:, b * SB_KEY_BLOCK:(b + 1) * SB_KEY_BLOCK] for b in range(nblk)]
    if stack_blocks and nblk > 1:
        stacked = _bdot(jnp.concatenate(blocks, axis=0).astype(BF16), suffix)
        within = [stacked[b * m:(b + 1) * m, :] for b in range(nblk)]
    else:
        within = [_bdot(blk.astype(BF16), suffix) for blk in blocks]
    later = [None] * nblk
    for b in reversed(range(nblk)):
        later[b] = within[b] + run
        run = run + (within[b][:, 0:1] + blocks[b][:, 0:1])
    later = later[0] if nblk == 1 else jnp.concatenate(later, axis=1)
    a = jnp.exp(z - sp - later)
    if mask is not None:
        a = jnp.where(mask, a, 0.0)
    a = a.astype(BF16).astype(vt.dtype)
    if keys_in_lanes:
        return lax.dot_general(a, vt, _NT, preferred_element_type=F32), run
    return _bdot(a, vt), run


def _head_stack(q, qm_ref, rows):
    lane_head = lax.broadcasted_iota(jnp.int32, q.shape, 1) // HEAD_DIM
    for h in range(N_HEADS):
        qm_ref[h * rows:(h + 1) * rows, :] = jnp.where(lane_head == h, q, 0.0).astype(BF16)


def _head_unstack(acc_ref, rows):
    lane_head = lax.broadcasted_iota(jnp.int32, (rows, N_HEADS * HEAD_DIM), 1) // HEAD_DIM
    out = jnp.where(lane_head == 0, acc_ref[0:rows, :], 0.0)
    for h in range(1, N_HEADS):
        out = out + jnp.where(lane_head == h, acc_ref[h * rows:(h + 1) * rows, :], 0.0)
    return out


def _attend_body(pt_ref, q_ref, kb_ref, vb_ref, bias_ref, qs_ref, kn_ref, vn_ref, bias_s_ref,
                 ck_hbm, cv_hbm, o_ref, os_ref, qm_ref, acc_ref, run_ref, qms_ref, kpg, vpg, sems,
                 *, tq, wide, layer, n_pages, npg, nsq, tpad):
    i = pl.program_id(1)
    step_idx = pl.program_id(0) * pl.num_programs(1) + i
    n_groups = n_pages // npg
    blk = SB_KEY_BLOCK
    suffix = _suffix_matrix()

    def page_copies(j, g):
        seq = step_idx * nsq + j
        copies = []
        for p in range(npg):
            page = pt_ref[seq * n_pages + g * npg + p]
            slot = (j * n_groups + g) * npg + p
            copies.append(pltpu.make_async_copy(ck_hbm.at[layer, page], kpg.at[slot],
                                                sems.at[0, j * n_groups + g]))
            copies.append(pltpu.make_async_copy(cv_hbm.at[layer, page], vpg.at[slot],
                                                sems.at[1, j * n_groups + g]))
        return copies

    for j in range(nsq):
        for g in range(n_groups):
            for copy in page_copies(j, g):
                copy.start()

    m = N_HEADS * tq
    _head_stack(q_ref[0] * (HEAD_DIM ** -0.5), qm_ref, tq)
    qm = qm_ref[...]
    bias_col = bias_ref[...]

    def keys(start, n):
        start = pl.multiple_of(start, blk)
        return kb_ref[0, pl.ds(start, n), :], vb_ref[0, pl.ds(start, n), :]

    t_q = lax.broadcasted_iota(jnp.int32, (m, tq), 0) & (tq - 1)
    t_k = lax.broadcasted_iota(jnp.int32, (m, tq), 1)
    pv, run = _sb_tile(qm, *keys(i * tq, tq), bias_col, jnp.zeros((m, 1), F32), suffix,
                       t_k < t_q, keys_in_lanes=False)
    acc_ref[...] = pv
    run_ref[...] = run

    def step(start, n):
        pv, run = _sb_tile(qm, *keys(start, n), bias_col, run_ref[...], suffix, None,
                           keys_in_lanes=False)
        acc_ref[...] += pv
        run_ref[...] = run

    per_wide = wide // blk
    jd = (i * tq) // blk
    n_single = jd % per_wide
    n_wide = jd // per_wide

    def single_body(n, carry):
        step((jd - 1 - n) * blk, blk)
        return carry

    def wide_body(n, carry):
        step((n_wide - 1 - n) * wide, wide)
        return carry

    lax.fori_loop(0, n_single, single_body, 0)
    lax.fori_loop(0, n_wide, wide_body, 0)
    o_ref[0] = _head_unstack(acc_ref, tq)

    ms = N_HEADS * tpad
    bias_s = bias_s_ref[...]
    s_q = lax.broadcasted_iota(jnp.int32, (ms, blk), 0) & (tpad - 1)
    s_k = lax.broadcasted_iota(jnp.int32, (ms, blk), 1)
    for j in range(nsq):
        for g in range(n_groups):
            for copy in page_copies(j, g):
                copy.wait()
    for j in range(nsq):
        qms = qms_ref.at[j * ms:(j + 1) * ms]
        _head_stack(qs_ref[j] * (HEAD_DIM ** -0.5), qms, tpad)
        kn = jnp.concatenate([kn_ref[j], jnp.zeros((blk - tpad, kn_ref.shape[2]), F32)], axis=0)
        vn = jnp.concatenate([vn_ref[j], jnp.zeros((blk - tpad, vn_ref.shape[2]), F32)], axis=0)
        acc, run = _sb_tile(qms[...], kn.astype(BF16), vn.astype(BF16), bias_s,
                            jnp.zeros((ms, 1), F32), suffix, s_k < s_q, keys_in_lanes=False)
        for g in reversed(range(n_groups)):
            first = (j * n_groups + g) * npg
            kt = jnp.concatenate([kpg[first + p] for p in range(npg)], axis=1)
            vt = jnp.concatenate([vpg[first + p] for p in range(npg)], axis=1)
            pv, run = _sb_tile(qms[...], kt, vt, bias_s, run, suffix, None,
                               keys_in_lanes=True, stack_blocks=True)
            acc = acc + pv
        os_ref[j] = _head_unstack(acc, tpad)


def _attend(q, kb, vb, qs, k_new, v_new, sb_bias, cache_kt, cache_vt, layer, page_table,
            *, tq=2 * SB_KEY_BLOCK, wide=2 * SB_KEY_BLOCK):
    B, S, w = q.shape
    n_seq, tpad, _ = qs.shape
    page = cache_kt.shape[3]
    n_pages = page_table.shape[1]
    tq = min(tq, S)
    nq = S // tq
    npg = min(PAGES_PER_GROUP, n_pages)
    assert tq % SB_KEY_BLOCK == 0 and tq & (tq - 1) == 0 and S % tq == 0
    assert n_seq % (B * nq) == 0 and n_pages % npg == 0 and (npg * page) % SB_KEY_BLOCK == 0
    nsq = n_seq // (B * nq)
    m, ms = N_HEADS * tq, N_HEADS * tpad
    n_slots = nsq * n_pages
    bias_rows = jnp.repeat(sb_bias.astype(F32), tq).reshape(m, 1)
    bias_rows_s = jnp.repeat(sb_bias.astype(F32), tpad).reshape(ms, 1)
    tile = pl.BlockSpec((1, tq, w), lambda b, i, pt: (b, i, 0))
    whole_seq = pl.BlockSpec((1, S, w), lambda b, i, pt: (b, 0, 0), pipeline_mode=pl.Buffered(1))
    per_step = pl.BlockSpec((nsq, tpad, w), lambda b, i, pt: (b * nq + i, 0, 0))
    const = lambda rows: pl.BlockSpec((rows, 1), lambda b, i, pt: (0, 0), pipeline_mode=pl.Buffered(1))
    hbm = pl.BlockSpec(memory_space=pl.ANY)
    grid_spec = pltpu.PrefetchScalarGridSpec(
        num_scalar_prefetch=1,
        grid=(B, nq),
        in_specs=[tile, whole_seq, whole_seq, const(m), per_step, per_step, per_step, const(ms),
                  hbm, hbm],
        out_specs=[tile, per_step],
        scratch_shapes=[pltpu.VMEM((m, w), BF16), pltpu.VMEM((m, w), F32), pltpu.VMEM((m, 1), F32),
                        pltpu.VMEM((nsq * ms, w), BF16),
                        pltpu.VMEM((n_slots, w, page), F32), pltpu.VMEM((n_slots, w, page), F32),
                        pltpu.SemaphoreType.DMA((2, nsq * (n_pages // npg)))],
    )
    return pl.pallas_call(
        functools.partial(_attend_body, tq=tq, wide=wide, layer=layer, n_pages=n_pages, npg=npg,
                          nsq=nsq, tpad=tpad),
        grid_spec=grid_spec,
        out_shape=[jax.ShapeDtypeStruct((B, S, w), F32), jax.ShapeDtypeStruct((n_seq, tpad, w), F32)],
        compiler_params=pltpu.CompilerParams(
            dimension_semantics=("arbitrary", "arbitrary"), vmem_limit_bytes=VMEM_LIMIT),
        name="attend",
    )(page_table.reshape(-1), q, kb, vb, bias_rows, qs, k_new, v_new, bias_rows_s, cache_kt, cache_vt)


def _merge_body(x_ref, oa_ref, ob_ref, oc_ref, od_ref, gmix_ref, wgate_ref, wbr_ref, wo_ref, xo_ref):
    x = x_ref[...]
    d = x.shape[1]
    h = _rms(x, gmix_ref[...]).astype(BF16)
    merged = None
    for kk, br_ref in enumerate((oa_ref, ob_ref, oc_ref, od_ref)):
        gate = jax.nn.sigmoid(_bdot(h, wgate_ref[0, :, kk * d:(kk + 1) * d]))
        term = gate * _bdot(br_ref[...].astype(BF16), wbr_ref[kk])
        merged = term if merged is None else merged + term
    xo_ref[...] = x + _bdot(merged.astype(BF16), wo_ref[...])


def _merge(x, oa, ob, oc, od, params, *, tm):
    n, d = x.shape
    w = BR_WIDTH
    tok = lambda width: pl.BlockSpec((tm, width), lambda i: (i, 0))
    consts = [a for a, _ in params]
    return pl.pallas_call(
        _merge_body,
        grid=(n // tm,),
        in_specs=[tok(d)] + [tok(w)] * 4 + [spec for _, spec in params],
        out_specs=tok(d),
        out_shape=jax.ShapeDtypeStruct((n, d), F32),
        compiler_params=pltpu.CompilerParams(
            dimension_semantics=("arbitrary",), vmem_limit_bytes=VMEM_LIMIT),
        name="merge",
    )(x, oa, ob, oc, od, *consts)


def _ffn_body(x_ref, p_ref, st_ref, gffn_ref, wup_ref, cw_ref, wdn_ref, gple_ref, wpg_ref,
              wple_ref, gfin_ref, xo_ref, nf_ref, ubuf, act_ref, *, tm, sh, hh, final):
    i = pl.program_id(1)

    @pl.when(i == 0)
    def _():
        ubuf[0:hh, :] = st_ref[0]

    x = x_ref[0]
    hn = _rms(x, gffn_ref[...]).astype(BF16)
    d_ff = wdn_ref.shape[0]
    cw = FFN_CHUNK
    ubuf[hh:hh + tm, :] = _bdot(hn, wup_ref[...])
    base = hh - (FFN_TAPS - 1) * sh
    for c in range(d_ff // cw):
        conv = []
        for col in (c * cw, d_ff + c * cw):
            cv = cw_ref[0:1, col:col + cw] * ubuf[base:base + tm, col:col + cw]
            for j in range(1, FFN_TAPS):
                cv = cv + cw_ref[j:j + 1, col:col + cw] * ubuf[base + j * sh:base + j * sh + tm, col:col + cw]
            conv.append(cv)
        act_ref[:, c * cw:(c + 1) * cw] = (jax.nn.silu(conv[0]) * conv[1]).astype(BF16)
    nf_ref[0] = ubuf[hh + tm - (FFN_TAPS - 1) * sh:hh + tm, :]
    ubuf[0:hh, :] = ubuf[tm:tm + hh, :]
    x = x + _bdot(act_ref[...], wdn_ref[...])
    gate = jax.nn.sigmoid(_bdot(_rms(x, gple_ref[...]).astype(BF16), wpg_ref[...]))
    x = x + gate * _bdot(p_ref[0].astype(BF16), wple_ref[...])
    xo_ref[0] = _rms(x, gfin_ref[...]) if final else x


def _ffn(x, p, st_f, params, *, tm, sh, final):
    B, S, D = x.shape
    hh, up_cols = st_f.shape[1], st_f.shape[2]
    tok = lambda width: pl.BlockSpec((1, tm, width), lambda b, i: (b, i, 0))
    consts = [a for a, _ in params]
    p, p_spec = p
    nf_rows = (FFN_TAPS - 1) * sh
    return pl.pallas_call(
        functools.partial(_ffn_body, tm=tm, sh=sh, hh=hh, final=final),
        grid=(B, S // tm),
        in_specs=[tok(D), p_spec, pl.BlockSpec((1, hh, up_cols), lambda b, i: (b, 0, 0))]
                 + [spec for _, spec in params],
        out_specs=[tok(D), pl.BlockSpec((1, nf_rows, up_cols), lambda b, i: (b, 0, 0))],
        out_shape=[jax.ShapeDtypeStruct((B, S, D), F32),
                   jax.ShapeDtypeStruct((B, nf_rows, up_cols), F32)],
        scratch_shapes=[pltpu.VMEM((hh + tm, up_cols), F32), pltpu.VMEM((tm, up_cols // 2), BF16)],
        compiler_params=pltpu.CompilerParams(
            dimension_semantics=("arbitrary", "arbitrary"), vmem_limit_bytes=VMEM_LIMIT),
        name="ffn",
    )(x, p, st_f, *consts)


def _token_major(a):
    n_seq, t, c = a.shape
    return a.transpose(1, 0, 2).reshape(1, t * n_seq, c)


def _seq_major(a, n_seq):
    c = a.shape[-1]
    return a.reshape(-1, n_seq, c).transpose(1, 0, 2)


def kernel(x_prompt, x_sample, p_prompt, p_sample, cache_k, cache_v, page_table, state_conv_b, state_conv_c, state_ffn_conv, g_mix, w_in, ln_v_g, ln_v_b, w_sp, b_sp, conv_b_w, conv_c_w, ln_c_g, ln_c_b, sb_bias, w_br, w_o, g_ffn, w_up, ffn_conv_w, w_down, g_ple, w_ple_gate, w_ple, g_final):
    depth = w_in.shape[0]
    B, S, D = x_prompt.shape
    n_seq, T, _ = x_sample.shape
    w = BR_WIDTH
    n_grp = w_sp.shape[1]
    chunk = w_sp.shape[2]
    tm = min(TOKEN_TILE, S)
    tpad = SUBLANES
    assert S % tm == 0 and tm % chunk == 0 and S % SB_KEY_BLOCK == 0 and T <= tpad

    xp = x_prompt
    xs = _token_major(x_sample)
    tile_rows = lambda taps: -(-(taps - 1) // SUBLANES) * SUBLANES
    zeros_b = jnp.zeros((B, tile_rows(CONV_B_TAPS), w), F32)
    zeros_c = jnp.zeros((B, tile_rows(CONV_C_TAPS), w), F32)
    zeros_f = jnp.zeros((B, tile_rows(FFN_TAPS), w_up.shape[2]), F32)
    n_phys, page = cache_k.shape[1], cache_k.shape[2]
    cache_kt = cache_k.transpose(0, 1, 3, 4, 2).reshape(depth, n_phys, w, page)
    cache_vt = cache_v.transpose(0, 1, 3, 4, 2).reshape(depth, n_phys, w, page)

    w_in_b = w_in.astype(BF16)
    w_br_b = w_br.astype(BF16)
    w_o_b = w_o.astype(BF16)
    w_up_b = w_up.astype(BF16)
    w_down_b = w_down.astype(BF16)
    w_pg_b = w_ple_gate.astype(BF16)
    w_ple_b = w_ple.astype(BF16)
    grp_cols = w // n_grp
    ple_dim = p_prompt.shape[-1]

    outs = {name: [] for name in ("cbp", "ccp", "cfp", "ks", "vs", "cvs", "cbs", "ccs", "cfs")}
    kv_prompt = None
    for l in range(depth):
        final = l == depth - 1
        lp = functools.partial(_layer_param, layer=l)
        mixer_shared = [lp(g_mix), lp(w_in_b, cols=(0, MIX_COLS)), lp(ln_v_g), lp(ln_v_b)]
        mixer_conv = [lp(conv_b_w), lp(conv_c_w), lp(ln_c_g), lp(ln_c_b)]
        merge_params = [lp(g_mix), lp(w_in_b, cols=(MIX_COLS, w_in.shape[2] - MIX_COLS)),
                        lp(w_br_b), lp(w_o_b)]
        ffn_params = [lp(g_ffn), lp(w_up_b), lp(ffn_conv_w), lp(w_down_b), lp(g_ple), lp(w_pg_b),
                      lp(w_ple_b), _whole(g_final.reshape(1, -1))]

        bmat = jnp.repeat(b_sp[l][:, :chunk].T, grp_cols, axis=1)
        (oa, ob, oc, q, k, v, kb, vb, _, nb, nc) = _mixer(
            xp, mixer_shared + [lp(w_sp), _whole(bmat)] + mixer_conv, zeros_b, zeros_c,
            sample=False, tm=tm, sh=1, kv_stack=(l, depth, kv_prompt))
        kv_prompt = (k, v)
        outs["cbp"].append(nb)
        outs["ccp"].append(nc)

        wv = jnp.repeat(w_sp[l][:, :T, :T].transpose(1, 2, 0).reshape(T * T, n_grp), grp_cols, axis=1)
        bv = jnp.repeat(b_sp[l][:, :T].T, grp_cols, axis=1)
        (oa_s, ob_s, oc_s, q_s, k, v, _, _, va, nb, nc) = _mixer(
            xs, mixer_shared + [_whole(wv), _whole(bv)] + mixer_conv,
            _token_major(state_conv_b[l]), _token_major(state_conv_c[l]),
            sample=True, tm=T * n_seq, sh=n_seq)

        pad_t = lambda a: jnp.pad(_seq_major(a, n_seq), ((0, 0), (0, tpad - T), (0, 0)))
        od, od_s = _attend(q, kb, vb, pad_t(q_s), pad_t(k), pad_t(v), sb_bias[l], cache_kt, cache_vt,
                           l, page_table)
        od_s = _token_major(od_s[:, :T, :])

        flat = lambda a: a.reshape(B * S, a.shape[-1])
        x1 = _merge(flat(xp), flat(oa), flat(ob), flat(oc), flat(od), merge_params,
                    tm=tm).reshape(B, S, D)
        p_spec = pl.BlockSpec((None, 1, tm, ple_dim), lambda b, i, l=l: (l, b, i, 0))
        xp, nf = _ffn(x1, (p_prompt, p_spec), zeros_f, ffn_params, tm=tm, sh=1, final=final)
        outs["cfp"].append(nf)

        x1 = _merge(xs[0], oa_s[0], ob_s[0], oc_s[0], od_s[0], merge_params, tm=T * n_seq)[None]
        ps = _token_major(p_sample[l])
        p_spec = pl.BlockSpec((1, T * n_seq, ple_dim), lambda b, i: (b, i, 0))
        xs, nf = _ffn(x1, (ps, p_spec), _token_major(state_ffn_conv[l]), ffn_params,
                      tm=T * n_seq, sh=n_seq, final=final)
        outs["ks"].append(_seq_major(k, n_seq).reshape(n_seq, T, N_HEADS, HEAD_DIM))
        outs["vs"].append(_seq_major(v, n_seq).reshape(n_seq, T, N_HEADS, HEAD_DIM))
        outs["cvs"].append(_seq_major(va, n_seq))
        outs["cbs"].append(_seq_major(nb, n_seq))
        outs["ccs"].append(_seq_major(nc, n_seq))
        outs["cfs"].append(_seq_major(nf, n_seq))

    st = lambda name: jnp.stack(outs[name])
    heads_last = lambda t: t.reshape(depth, B, N_HEADS, HEAD_DIM, S).transpose(0, 1, 4, 2, 3)
    return (xp, _seq_major(xs, n_seq), heads_last(kv_prompt[0]), heads_last(kv_prompt[1]),
            st("cbp"), st("ccp"), st("cfp"),
            st("ks"), st("vs"), st("cvs"), st("cbs"), st("ccs"), st("cfs"))
```

```python
import functools

import jax
import jax.numpy as jnp
from jax import lax
from jax.experimental import pallas as pl
from jax.experimental.pallas import tpu as pltpu

F32 = jnp.float32
BF16 = jnp.bfloat16

EPS = 1e-6
SUBLANES = 8
BR_WIDTH = 256
N_HEADS = 4
HEAD_DIM = 64
MIX_COLS = 10 * BR_WIDTH
CONV_B_TAPS = 3
CONV_C_TAPS = 31
FFN_TAPS = 3
FFN_CHUNK = 256
TOKEN_TILE = 512
SB_KEY_BLOCK = 256
PAGES_PER_GROUP = 64
WIDE_UNROLL = 4
VMEM_LIMIT = 56 * 1024 * 1024


def _rms(x, g):
    return x * lax.rsqrt(jnp.mean(x * x, axis=-1, keepdims=True) + EPS) * g


def _layernorm(x, g, b):
    mu = jnp.mean(x, axis=-1, keepdims=True)
    xc = x - mu
    var = jnp.mean(xc * xc, axis=-1, keepdims=True)
    return xc * lax.rsqrt(var + EPS) * g + b


def _bdot(a, b):
    return jnp.dot(a, b, preferred_element_type=F32)


def _const_spec(shape):
    nd = len(shape)
    return pl.BlockSpec(shape, lambda *_: (0,) * nd, pipeline_mode=pl.Buffered(1))


def _whole(a):
    return a, _const_spec(a.shape)


def _layer_param(a, layer, cols=None):
    if a.ndim == 2:
        a = a[:, None, :]
    if cols is None:
        block = (None,) + a.shape[1:]
        index = (layer,) + (0,) * (a.ndim - 1)
    else:
        block = (pl.Element(1),) + tuple(pl.Element(n) for n in a.shape[1:-1]) + (pl.Element(cols[1]),)
        index = (layer,) + (0,) * (a.ndim - 2) + (cols[0],)
    return a, pl.BlockSpec(block, lambda *_: index, pipeline_mode=pl.Buffered(1))


def _mixer_body(x_ref, gmix_ref, wmix_ref, lnvg_ref, lnvb_ref, wsp_ref, bsp_ref,
                cbw_ref, ccw_ref, lncg_ref, lncb_ref, stb_ref, stc_ref,
                oa_ref, ob_ref, oc_ref, q_ref, k_ref, v_ref, kb_ref, vb_ref,
                va_ref, nb_ref, nc_ref, bbuf, cbuf, wbuf, *, sample, tm, sh, hb, hc, kv_fill):
    i = pl.program_id(1)

    @pl.when(i == 0)
    def _():
        bbuf[0:hb, :] = stb_ref[0]
        cbuf[0:hc, :] = stc_ref[0]

    w = BR_WIDTH
    h = _rms(x_ref[0], gmix_ref[...]).astype(BF16)
    y = _bdot(h, wmix_ref[0])

    ga = jax.nn.gelu(y[:, 0:2 * w])
    u = ga[:, 0:w]
    va = _layernorm(ga[:, w:2 * w], lnvg_ref[...], lnvb_ref[...])
    va_ref[0] = va
    parts = []
    if sample:
        nt = tm // sh
        for t in range(nt):
            s = jnp.broadcast_to(bsp_ref[t:t + 1, :], (sh, w))
            for s2 in range(t + 1):
                s = s + wsp_ref[t * nt + s2:t * nt + s2 + 1, :] * va[s2 * sh:(s2 + 1) * sh, :]
            parts.append(s)
    else:
        ch = wsp_ref.shape[1]
        rr = lax.broadcasted_iota(jnp.int32, (ch, ch), 0)
        cc = lax.broadcasted_iota(jnp.int32, (ch, ch), 1)
        grp = lax.broadcasted_iota(jnp.int32, (ch, w), 1) // (w // wsp_ref.shape[0])
        wg = [jnp.where(cc <= rr, wsp_ref[g], 0.0).astype(BF16) for g in range(wsp_ref.shape[0])]
        for c in range(tm // ch):
            vc = va[c * ch:(c + 1) * ch, :]
            s = bsp_ref[...]
            for g in range(wsp_ref.shape[0]):
                s = s + _bdot(wg[g], jnp.where(grp == g, vc, 0.0).astype(BF16))
            parts.append(s)
    oa_ref[0] = u * jnp.concatenate(parts, axis=0)

    bbuf[hb:hb + tm, :] = y[:, 3 * w:4 * w] * y[:, 4 * w:5 * w]
    base = hb - (CONV_B_TAPS - 1) * sh
    cb = cbw_ref[0:1, :] * bbuf[base:base + tm, :]
    for j in range(1, CONV_B_TAPS):
        cb = cb + cbw_ref[j:j + 1, :] * bbuf[base + j * sh:base + j * sh + tm, :]
    ob_ref[0] = y[:, 2 * w:3 * w] * cb
    nb_ref[0] = bbuf[hb + tm - (CONV_B_TAPS - 1) * sh:hb + tm, :]
    bbuf[0:hb, :] = bbuf[tm:tm + hb, :]

    cbuf[hc:hc + tm, :] = y[:, 5 * w:6 * w] * jax.nn.sigmoid(y[:, 6 * w:7 * w])
    base = hc - (CONV_C_TAPS - 1) * sh
    cv = None
    residues = SUBLANES if sh % SUBLANES else 1
    for b in range(residues):
        taps = list(range(b, CONV_C_TAPS, residues))
        span = (taps[-1] - b) * sh
        start = base + b * sh
        if start % SUBLANES:
            win = wbuf.at[b]
            win[0:span + tm, :] = cbuf[start:start + span + tm, :]
            start = 0
        else:
            win = cbuf
        for j in taps:
            off = start + (j - b) * sh
            term = ccw_ref[j:j + 1, :] * win[off:off + tm, :]
            cv = term if cv is None else cv + term
    oc_ref[0] = jax.nn.silu(_layernorm(cv, lncg_ref[...], lncb_ref[...]))
    nc_ref[0] = cbuf[hc + tm - (CONV_C_TAPS - 1) * sh:hc + tm, :]
    cbuf[0:hc, :] = cbuf[tm:tm + hc, :]

    q = y[:, 7 * w:8 * w]
    k = y[:, 8 * w:9 * w]
    v = y[:, 9 * w:10 * w]
    q_ref[0] = q
    if sample:
        k_ref[0] = k
        v_ref[0] = v
    else:
        k_ref[0, 0] = k.T
        v_ref[0, 0] = v.T
        for d in range(1, 1 + kv_fill):
            k_ref[d, 0] = jnp.zeros((w, tm), F32)
            v_ref[d, 0] = jnp.zeros((w, tm), F32)
    kb_ref[0] = k.astype(BF16)
    vb_ref[0] = v.astype(BF16)


def _without_refs(body, at, n):
    def wrapped(*refs, **kwargs):
        return body(*refs[:at], *refs[at + n:], **kwargs)
    return wrapped


def _mixer(x, params, st_b, st_c, *, sample, tm, sh, kv_stack=None):
    B, S, D = x.shape
    w = BR_WIDTH
    hb, hc = st_b.shape[1], st_c.shape[1]
    nt = S // tm
    tok = lambda width: pl.BlockSpec((1, tm, width), lambda b, i: (b, i, 0))
    per_seq = lambda rows, width: pl.BlockSpec((1, rows, width), lambda b, i: (b, 0, 0))
    consts = [a for a, _ in params]
    f32_tok = jax.ShapeDtypeStruct((B, S, w), F32)
    bf_tok = jax.ShapeDtypeStruct((B, S, w), BF16)
    realign_shape = (SUBLANES, tm + CONV_C_TAPS - 1, w) if sh % SUBLANES else (1, SUBLANES, w)
    stacks, stack_specs, aliases, kv_fill = [], [], {}, 0
    if sample:
        kv_spec, kv_shape = tok(w), f32_tok
    else:
        layer, depth, previous = kv_stack
        kv_shape = jax.ShapeDtypeStruct((depth, B, w, S), F32)
        if previous is None:
            assert layer == 0
            kv_fill = depth - 1
            kv_spec = pl.BlockSpec((depth, 1, w, tm), lambda b, i: (0, b, 0, i))
        else:
            kv_spec = pl.BlockSpec((1, 1, w, tm), lambda b, i: (layer, b, 0, i))
            n_in = 1 + len(consts) + 2
            stacks, stack_specs = list(previous), [pl.BlockSpec(memory_space=pl.ANY)] * 2
            aliases = {n_in: 4, n_in + 1: 5}
    body = functools.partial(_mixer_body, sample=sample, tm=tm, sh=sh, hb=hb, hc=hc, kv_fill=kv_fill)
    if stacks:
        body = _without_refs(body, n_in, 2)
    nb_rows, nc_rows = (CONV_B_TAPS - 1) * sh, (CONV_C_TAPS - 1) * sh
    return pl.pallas_call(
        body,
        grid=(B, nt),
        in_specs=[tok(D)] + [spec for _, spec in params] + [per_seq(hb, w), per_seq(hc, w)]
                 + stack_specs,
        input_output_aliases=aliases,
        out_specs=[tok(w)] * 4 + [kv_spec] * 2 + [tok(w)] * 3
                  + [per_seq(nb_rows, w), per_seq(nc_rows, w)],
        out_shape=[f32_tok] * 4 + [kv_shape] * 2 + [bf_tok] * 2 + [f32_tok]
                  + [jax.ShapeDtypeStruct((B, nb_rows, w), F32),
                     jax.ShapeDtypeStruct((B, nc_rows, w), F32)],
        scratch_shapes=[pltpu.VMEM((hb + tm, w), F32), pltpu.VMEM((hc + tm, w), F32),
                        pltpu.VMEM(realign_shape, F32)],
        compiler_params=pltpu.CompilerParams(
            dimension_semantics=("arbitrary", "arbitrary"), vmem_limit_bytes=VMEM_LIMIT),
        name="mixer_sample" if sample else "mixer_prompt",
    )(x, *consts, st_b, st_c, *stacks)


def _suffix_matrix():
    n = SB_KEY_BLOCK
    r = lax.broadcasted_iota(jnp.int32, (n, n), 0)
    c = lax.broadcasted_iota(jnp.int32, (n, n), 1)
    return (r > c).astype(BF16)


_NT = (((1,), (1,)), ((), ()))


def _neg_abs(x):
    return pltpu.bitcast(pltpu.bitcast(x, jnp.uint32) | jnp.uint32(0x80000000), F32)


def _sb_tile(qm, kt, vt, bias_col, run, suffix, mask, *, keys_in_lanes, stack_blocks=False):
    if keys_in_lanes:
        z = _bdot(qm.astype(kt.dtype), kt) + bias_col
    else:
        z = lax.dot_general(qm.astype(kt.dtype), kt, _NT, preferred_element_type=F32) + bias_col
    sp = jnp.maximum(z, 0.0) + jnp.log(1.0 + jnp.exp(_neg_abs(z)))
    if mask is not None:
        sp = jnp.where(mask, sp, 0.0)
    m = z.shape[0]
    nblk = z.shape[1] // SB_KEY_BLOCK
    blocks = [sp[:, b * SB_KEY_BLOCK:(b + 1) * SB_KEY_BLOCK] for b in range(nblk)]
    if stack_blocks and nblk > 1:
        stacked = _bdot(jnp.concatenate(blocks, axis=0).astype(BF16), suffix)
        within = [stacked[b * m:(b + 1) * m, :] for b in range(nblk)]
    else:
        within = [_bdot(blk.astype(BF16), suffix) for blk in blocks]
    later = [None] * nblk
    for b in reversed(range(nblk)):
        later[b] = within[b] + run
        run = run + (within[b][:, 0:1] + blocks[b][:, 0:1])
    later = later[0] if nblk == 1 else jnp.concatenate(later, axis=1)
    a = jnp.exp(z - sp - later)
    if mask is not None:
        a = jnp.where(mask, a, 0.0)
    a = a.astype(BF16).astype(vt.dtype)
    if keys_in_lanes:
        return lax.dot_general(a, vt, _NT, preferred_element_type=F32), run
    return _bdot(a, vt), run


def _head_stack(q, qm_ref, rows):
    lane_head = lax.broadcasted_iota(jnp.int32, q.shape, 1) // HEAD_DIM
    for h in range(N_HEADS):
        qm_ref[h * rows:(h + 1) * rows, :] = jnp.where(lane_head == h, q, 0.0).astype(BF16)


def _head_unstack(acc_ref, rows):
    lane_head = lax.broadcasted_iota(jnp.int32, (rows, N_HEADS * HEAD_DIM), 1) // HEAD_DIM
    out = jnp.where(lane_head == 0, acc_ref[0:rows, :], 0.0)
    for h in range(1, N_HEADS):
        out = out + jnp.where(lane_head == h, acc_ref[h * rows:(h + 1) * rows, :], 0.0)
    return out


def _attend_body(pt_ref, q_ref, kb_ref, vb_ref, bias_ref, qs_ref, kn_ref, vn_ref, bias_s_ref,
                 ck_hbm, cv_hbm, o_ref, os_ref, qm_ref, acc_ref, run_ref, qms_ref, kpg, vpg, sems,
                 *, tq, wide, layer, n_pages, npg, nsq, tpad):
    i = pl.program_id(1)
    step_idx = pl.program_id(0) * pl.num_programs(1) + i
    n_groups = n_pages // npg
    blk = SB_KEY_BLOCK
    suffix = _suffix_matrix()

    def page_copies(j, g):
        seq = step_idx * nsq + j
        copies = []
        for p in range(npg):
            page = pt_ref[seq * n_pages + g * npg + p]
            slot = (j * n_groups + g) * npg + p
            copies.append(pltpu.make_async_copy(ck_hbm.at[layer, page], kpg.at[slot],
                                                sems.at[0, j * n_groups + g]))
            copies.append(pltpu.make_async_copy(cv_hbm.at[layer, page], vpg.at[slot],
                                                sems.at[1, j * n_groups + g]))
        return copies

    for j in range(nsq):
        for g in range(n_groups):
            for copy in page_copies(j, g):
                copy.start()

    m = N_HEADS * tq
    _head_stack(q_ref[0] * (HEAD_DIM ** -0.5), qm_ref, tq)
    qm = qm_ref[...]
    bias_col = bias_ref[...]

    def keys(start, n):
        start = pl.multiple_of(start, blk)
        return kb_ref[0, pl.ds(start, n), :], vb_ref[0, pl.ds(start, n), :]

    t_q = lax.broadcasted_iota(jnp.int32, (m, tq), 0) & (tq - 1)
    t_k = lax.broadcasted_iota(jnp.int32, (m, tq), 1)
    pv, run = _sb_tile(qm, *keys(i * tq, tq), bias_col, jnp.zeros((m, 1), F32), suffix,
                       t_k < t_q, keys_in_lanes=False)
    acc_ref[...] = pv
    run_ref[...] = run

    def step(start, n):
        pv, run = _sb_tile(qm, *keys(start, n), bias_col, run_ref[...], suffix, None,
                           keys_in_lanes=False)
        acc_ref[...] += pv
        run_ref[...] = run

    per_wide = wide // blk
    jd = (i * tq) // blk
    n_single = jd % per_wide
    n_wide = jd // per_wide

    def single_body(n, carry):
        step((jd - 1 - n) * blk, blk)
        return carry

    def wide_tiles(count, below):
        for t in range(count):
            step((below + count - 1 - t) * wide, wide)

    def unrolled_body(n, carry):
        wide_tiles(WIDE_UNROLL, n_wide - WIDE_UNROLL * (n + 1))
        return carry

    lax.fori_loop(0, n_single, single_body, 0)
    lax.fori_loop(0, n_wide // WIDE_UNROLL, unrolled_body, 0)
    size = WIDE_UNROLL // 2
    while size:
        @pl.when((n_wide & size) != 0)
        def _(size=size):
            wide_tiles(size, n_wide & (size - 1))
        size //= 2
    o_ref[0] = _head_unstack(acc_ref, tq)

    ms = N_HEADS * tpad
    bias_s = bias_s_ref[...]
    s_q = lax.broadcasted_iota(jnp.int32, (ms, blk), 0) & (tpad - 1)
    s_k = lax.broadcasted_iota(jnp.int32, (ms, blk), 1)
    for j in range(nsq):
        for g in range(n_groups):
            for copy in page_copies(j, g):
                copy.wait()
    for j in range(nsq):
        qms = qms_ref.at[j * ms:(j + 1) * ms]
        _head_stack(qs_ref[j] * (HEAD_DIM ** -0.5), qms, tpad)
        kn = jnp.concatenate([kn_ref[j], jnp.zeros((blk - tpad, kn_ref.shape[2]), F32)], axis=0)
        vn = jnp.concatenate([vn_ref[j], jnp.zeros((blk - tpad, vn_ref.shape[2]), F32)], axis=0)
        acc, run = _sb_tile(qms[...], kn.astype(BF16), vn.astype(BF16), bias_s,
                            jnp.zeros((ms, 1), F32), suffix, s_k < s_q, keys_in_lanes=False)
        for g in reversed(range(n_groups)):
            first = (j * n_groups + g) * npg
            kt = jnp.concatenate([kpg[first + p] for p in range(npg)], axis=1)
            vt = jnp.concatenate([vpg[first + p] for p in range(npg)], axis=1)
            pv, run = _sb_tile(qms[...], kt, vt, bias_s, run, suffix, None,
                               keys_in_lanes=True, stack_blocks=True)
            acc = acc + pv
        os_ref[j] = _head_unstack(acc, tpad)


def _attend(q, kb, vb, qs, k_new, v_new, sb_bias, cache_kt, cache_vt, layer, page_table,
            *, tq=2 * SB_KEY_BLOCK, wide=2 * SB_KEY_BLOCK):
    B, S, w = q.shape
    n_seq, tpad, _ = qs.shape
    page = cache_kt.shape[3]
    n_pages = page_table.shape[1]
    tq = min(tq, S)
    nq = S // tq
    npg = min(PAGES_PER_GROUP, n_pages)
    assert tq % SB_KEY_BLOCK == 0 and tq & (tq - 1) == 0 and S % tq == 0
    assert n_seq % (B * nq) == 0 and n_pages % npg == 0 and (npg * page) % SB_KEY_BLOCK == 0
    nsq = n_seq // (B * nq)
    m, ms = N_HEADS * tq, N_HEADS * tpad
    n_slots = nsq * n_pages
    bias_rows = jnp.repeat(sb_bias.astype(F32), tq).reshape(m, 1)
    bias_rows_s = jnp.repeat(sb_bias.astype(F32), tpad).reshape(ms, 1)
    tile = pl.BlockSpec((1, tq, w), lambda b, i, pt: (b, i, 0))
    whole_seq = pl.BlockSpec((1, S, w), lambda b, i, pt: (b, 0, 0), pipeline_mode=pl.Buffered(1))
    per_step = pl.BlockSpec((nsq, tpad, w), lambda b, i, pt: (b * nq + i, 0, 0))
    const = lambda rows: pl.BlockSpec((rows, 1), lambda b, i, pt: (0, 0), pipeline_mode=pl.Buffered(1))
    hbm = pl.BlockSpec(memory_space=pl.ANY)
    grid_spec = pltpu.PrefetchScalarGridSpec(
        num_scalar_prefetch=1,
        grid=(B, nq),
        in_specs=[tile, whole_seq, whole_seq, const(m), per_step, per_step, per_step, const(ms),
                  hbm, hbm],
        out_specs=[tile, per_step],
        scratch_shapes=[pltpu.VMEM((m, w), BF16), pltpu.VMEM((m, w), F32), pltpu.VMEM((m, 1), F32),
                        pltpu.VMEM((nsq * ms, w), BF16),
                        pltpu.VMEM((n_slots, w, page), F32), pltpu.VMEM((n_slots, w, page), F32),
                        pltpu.SemaphoreType.DMA((2, nsq * (n_pages // npg)))],
    )
    return pl.pallas_call(
        functools.partial(_attend_body, tq=tq, wide=wide, layer=layer, n_pages=n_pages, npg=npg,
                          nsq=nsq, tpad=tpad),
        grid_spec=grid_spec,
        out_shape=[jax.ShapeDtypeStruct((B, S, w), F32), jax.ShapeDtypeStruct((n_seq, tpad, w), F32)],
        compiler_params=pltpu.CompilerParams(
            dimension_semantics=("arbitrary", "arbitrary"), vmem_limit_bytes=VMEM_LIMIT),
        name="attend",
    )(page_table.reshape(-1), q, kb, vb, bias_rows, qs, k_new, v_new, bias_rows_s, cache_kt, cache_vt)


def _merge_body(x_ref, oa_ref, ob_ref, oc_ref, od_ref, gmix_ref, wgate_ref, wbr_ref, wo_ref, xo_ref):
    x = x_ref[...]
    d = x.shape[1]
    h = _rms(x, gmix_ref[...]).astype(BF16)
    merged = None
    for kk, br_ref in enumerate((oa_ref, ob_ref, oc_ref, od_ref)):
        gate = jax.nn.sigmoid(_bdot(h, wgate_ref[0, :, kk * d:(kk + 1) * d]))
        term = gate * _bdot(br_ref[...].astype(BF16), wbr_ref[kk])
        merged = term if merged is None else merged + term
    xo_ref[...] = x + _bdot(merged.astype(BF16), wo_ref[...])


def _merge(x, oa, ob, oc, od, params, *, tm):
    n, d = x.shape
    w = BR_WIDTH
    tok = lambda width: pl.BlockSpec((tm, width), lambda i: (i, 0))
    consts = [a for a, _ in params]
    return pl.pallas_call(
        _merge_body,
        grid=(n // tm,),
        in_specs=[tok(d)] + [tok(w)] * 4 + [spec for _, spec in params],
        out_specs=tok(d),
        out_shape=jax.ShapeDtypeStruct((n, d), F32),
        compiler_params=pltpu.CompilerParams(
            dimension_semantics=("arbitrary",), vmem_limit_bytes=VMEM_LIMIT),
        name="merge",
    )(x, oa, ob, oc, od, *consts)


def _ffn_body(x_ref, p_ref, st_ref, gffn_ref, wup_ref, cw_ref, wdn_ref, gple_ref, wpg_ref,
              wple_ref, gfin_ref, xo_ref, nf_ref, ubuf, act_ref, *, tm, sh, hh, final):
    i = pl.program_id(1)

    @pl.when(i == 0)
    def _():
        ubuf[0:hh, :] = st_ref[0]

    x = x_ref[0]
    hn = _rms(x, gffn_ref[...]).astype(BF16)
    d_ff = wdn_ref.shape[0]
    cw = FFN_CHUNK
    ubuf[hh:hh + tm, :] = _bdot(hn, wup_ref[...])
    base = hh - (FFN_TAPS - 1) * sh
    for c in range(d_ff // cw):
        conv = []
        for col in (c * cw, d_ff + c * cw):
            cv = cw_ref[0:1, col:col + cw] * ubuf[base:base + tm, col:col + cw]
            for j in range(1, FFN_TAPS):
                cv = cv + cw_ref[j:j + 1, col:col + cw] * ubuf[base + j * sh:base + j * sh + tm, col:col + cw]
            conv.append(cv)
        act_ref[:, c * cw:(c + 1) * cw] = (jax.nn.silu(conv[0]) * conv[1]).astype(BF16)
    nf_ref[0] = ubuf[hh + tm - (FFN_TAPS - 1) * sh:hh + tm, :]
    ubuf[0:hh, :] = ubuf[tm:tm + hh, :]
    x = x + _bdot(act_ref[...], wdn_ref[...])
    gate = jax.nn.sigmoid(_bdot(_rms(x, gple_ref[...]).astype(BF16), wpg_ref[...]))
    x = x + gate * _bdot(p_ref[0].astype(BF16), wple_ref[...])
    xo_ref[0] = _rms(x, gfin_ref[...]) if final else x


def _ffn(x, p, st_f, params, *, tm, sh, final):
    B, S, D = x.shape
    hh, up_cols = st_f.shape[1], st_f.shape[2]
    tok = lambda width: pl.BlockSpec((1, tm, width), lambda b, i: (b, i, 0))
    consts = [a for a, _ in params]
    p, p_spec = p
    nf_rows = (FFN_TAPS - 1) * sh
    return pl.pallas_call(
        functools.partial(_ffn_body, tm=tm, sh=sh, hh=hh, final=final),
        grid=(B, S // tm),
        in_specs=[tok(D), p_spec, pl.BlockSpec((1, hh, up_cols), lambda b, i: (b, 0, 0))]
                 + [spec for _, spec in params],
        out_specs=[tok(D), pl.BlockSpec((1, nf_rows, up_cols), lambda b, i: (b, 0, 0))],
        out_shape=[jax.ShapeDtypeStruct((B, S, D), F32),
                   jax.ShapeDtypeStruct((B, nf_rows, up_cols), F32)],
        scratch_shapes=[pltpu.VMEM((hh + tm, up_cols), F32), pltpu.VMEM((tm, up_cols // 2), BF16)],
        compiler_params=pltpu.CompilerParams(
            dimension_semantics=("arbitrary", "arbitrary"), vmem_limit_bytes=VMEM_LIMIT),
        name="ffn",
    )(x, p, st_f, *consts)


def _token_major(a):
    n_seq, t, c = a.shape
    return a.transpose(1, 0, 2).reshape(1, t * n_seq, c)


def _seq_major(a, n_seq):
    c = a.shape[-1]
    return a.reshape(-1, n_seq, c).transpose(1, 0, 2)


def kernel(x_prompt, x_sample, p_prompt, p_sample, cache_k, cache_v, page_table, state_conv_b, state_conv_c, state_ffn_conv, g_mix, w_in, ln_v_g, ln_v_b, w_sp, b_sp, conv_b_w, conv_c_w, ln_c_g, ln_c_b, sb_bias, w_br, w_o, g_ffn, w_up, ffn_conv_w, w_down, g_ple, w_ple_gate, w_ple, g_final):
    depth = w_in.shape[0]
    B, S, D = x_prompt.shape
    n_seq, T, _ = x_sample.shape
    w = BR_WIDTH
    n_grp = w_sp.shape[1]
    chunk = w_sp.shape[2]
    tm = min(TOKEN_TILE, S)
    tpad = SUBLANES
    assert S % tm == 0 and tm % chunk == 0 and S % SB_KEY_BLOCK == 0 and T <= tpad

    xp = x_prompt
    xs = _token_major(x_sample)
    tile_rows = lambda taps: -(-(taps - 1) // SUBLANES) * SUBLANES
    zeros_b = jnp.zeros((B, tile_rows(CONV_B_TAPS), w), F32)
    zeros_c = jnp.zeros((B, tile_rows(CONV_C_TAPS), w), F32)
    zeros_f = jnp.zeros((B, tile_rows(FFN_TAPS), w_up.shape[2]), F32)
    n_phys, page = cache_k.shape[1], cache_k.shape[2]
    cache_kt = cache_k.transpose(0, 1, 3, 4, 2).reshape(depth, n_phys, w, page)
    cache_vt = cache_v.transpose(0, 1, 3, 4, 2).reshape(depth, n_phys, w, page)

    w_in_b = w_in.astype(BF16)
    w_br_b = w_br.astype(BF16)
    w_o_b = w_o.astype(BF16)
    w_up_b = w_up.astype(BF16)
    w_down_b = w_down.astype(BF16)
    w_pg_b = w_ple_gate.astype(BF16)
    w_ple_b = w_ple.astype(BF16)
    grp_cols = w // n_grp
    ple_dim = p_prompt.shape[-1]

    outs = {name: [] for name in ("cbp", "ccp", "cfp", "ks", "vs", "cvs", "cbs", "ccs", "cfs")}
    kv_prompt = None
    for l in range(depth):
        final = l == depth - 1
        lp = functools.partial(_layer_param, layer=l)
        mixer_shared = [lp(g_mix), lp(w_in_b, cols=(0, MIX_COLS)), lp(ln_v_g), lp(ln_v_b)]
        mixer_conv = [lp(conv_b_w), lp(conv_c_w), lp(ln_c_g), lp(ln_c_b)]
        merge_params = [lp(g_mix), lp(w_in_b, cols=(MIX_COLS, w_in.shape[2] - MIX_COLS)),
                        lp(w_br_b), lp(w_o_b)]
        ffn_params = [lp(g_ffn), lp(w_up_b), lp(ffn_conv_w), lp(w_down_b), lp(g_ple), lp(w_pg_b),
                      lp(w_ple_b), _whole(g_final.reshape(1, -1))]

        bmat = jnp.repeat(b_sp[l][:, :chunk].T, grp_cols, axis=1)
        (oa, ob, oc, q, k, v, kb, vb, _, nb, nc) = _mixer(
            xp, mixer_shared + [lp(w_sp), _whole(bmat)] + mixer_conv, zeros_b, zeros_c,
            sample=False, tm=tm, sh=1, kv_stack=(l, depth, kv_prompt))
        kv_prompt = (k, v)
        outs["cbp"].append(nb)
        outs["ccp"].append(nc)

        wv = jnp.repeat(w_sp[l][:, :T, :T].transpose(1, 2, 0).reshape(T * T, n_grp), grp_cols, axis=1)
        bv = jnp.repeat(b_sp[l][:, :T].T, grp_cols, axis=1)
        (oa_s, ob_s, oc_s, q_s, k, v, _, _, va, nb, nc) = _mixer(
            xs, mixer_shared + [_whole(wv), _whole(bv)] + mixer_conv,
            _token_major(state_conv_b[l]), _token_major(state_conv_c[l]),
            sample=True, tm=T * n_seq, sh=n_seq)

        pad_t = lambda a: jnp.pad(_seq_major(a, n_seq), ((0, 0), (0, tpad - T), (0, 0)))
        od, od_s = _attend(q, kb, vb, pad_t(q_s), pad_t(k), pad_t(v), sb_bias[l], cache_kt, cache_vt,
                           l, page_table)
        od_s = _token_major(od_s[:, :T, :])

        flat = lambda a: a.reshape(B * S, a.shape[-1])
        x1 = _merge(flat(xp), flat(oa), flat(ob), flat(oc), flat(od), merge_params,
                    tm=tm).reshape(B, S, D)
        p_spec = pl.BlockSpec((None, 1, tm, ple_dim), lambda b, i, l=l: (l, b, i, 0))
        xp, nf = _ffn(x1, (p_prompt, p_spec), zeros_f, ffn_params, tm=tm, sh=1, final=final)
        outs["cfp"].append(nf)

        x1 = _merge(xs[0], oa_s[0], ob_s[0], oc_s[0], od_s[0], merge_params, tm=T * n_seq)[None]
        ps = _token_major(p_sample[l])
        p_spec = pl.BlockSpec((1, T * n_seq, ple_dim), lambda b, i: (b, i, 0))
        xs, nf = _ffn(x1, (ps, p_spec), _token_major(state_ffn_conv[l]), ffn_params,
                      tm=T * n_seq, sh=n_seq, final=final)
        outs["ks"].append(_seq_major(k, n_seq).reshape(n_seq, T, N_HEADS, HEAD_DIM))
        outs["vs"].append(_seq_major(v, n_seq).reshape(n_seq, T, N_HEADS, HEAD_DIM))
        outs["cvs"].append(_seq_major(va, n_seq))
        outs["cbs"].append(_seq_major(nb, n_seq))
        outs["ccs"].append(_seq_major(nc, n_seq))
        outs["cfs"].append(_seq_major(nf, n_seq))

    st = lambda name: jnp.stack(outs[name])
    heads_last = lambda t: t.reshape(depth, B, N_HEADS, HEAD_DIM, S).transpose(0, 1, 4, 2, 3)
    return (xp, _seq_major(xs, n_seq), heads_last(kv_prompt[0]), heads_last(kv_prompt[1]),
            st("cbp"), st("ccp"), st("cfp"),
            st("ks"), st("vs"), st("cvs"), st("cbs"), st("ccs"), st("cfs"))
```
